```python
import jax, jax.numpy as jnp
from jax import lax
import numpy as np

D_MODEL = 4096
BATCH = 2
SEQ = 4096
DEPTH = 2
DEC_BATCH = 16
DEC_SEQ = 64
PAST_LEN = 2048

CHUNK = 64
N_HEADS = 16
N_KV = 4
HEAD_DIM = 128
WINDOW = 128
SWA_CHUNKS = WINDOW // CHUNK
ROPE_THETA = 10000.0
GLA_HEADS = 4
GLA_DK = 256
GLA_DV = 512
GLA_LOWRANK = 16
GLA_TAU = 16.0
LRU_WIDTH = 2048
LRU_BLOCKS = 16
LRU_BW = LRU_WIDTH // LRU_BLOCKS
CONV_W = 4
LRU_C = 8.0
N_BRANCH = 3
BRANCH_W = 2048
FFN_DIM = 14336
N_EXPERTS = 8
TOP_K = 2
EPS = 1e-6
NEG_INF = -1e30
F32 = jnp.float32

A_Q = N_HEADS * HEAD_DIM
A_KV = N_KV * HEAD_DIM
G_QK = GLA_HEADS * GLA_DK
G_V = GLA_HEADS * GLA_DV
SPLIT_SIZES = (A_Q, A_KV, A_KV, G_QK, G_QK, G_V, G_V, GLA_LOWRANK, LRU_WIDTH, LRU_WIDTH)
IN_COLS = sum(SPLIT_SIZES)
SPLIT_POINTS = tuple(int(s) for s in np.cumsum(SPLIT_SIZES)[:-1])

kernel_name = "hybrid_stream_encoder_step"


def rms_norm(x, g):
    xf = x.astype(F32)
    y = xf * lax.rsqrt(jnp.mean(xf * xf, axis=-1, keepdims=True) + EPS)
    return (y * g.astype(F32)).astype(x.dtype)


def rope(x, pos):
    half = HEAD_DIM // 2
    inv = ROPE_THETA ** (-jnp.arange(half, dtype=F32) / half)
    ang = pos.astype(F32)[:, None] * inv[None, :]
    cos = jnp.cos(ang)[None, :, None, :]
    sin = jnp.sin(ang)[None, :, None, :]
    xf = x.astype(F32)
    x1, x2 = xf[..., :half], xf[..., half:]
    return jnp.concatenate([x1 * cos - x2 * sin, x2 * cos + x1 * sin], axis=-1).astype(x.dtype)


def sink_softmax(s, sink):
    sk = jnp.broadcast_to(sink.astype(F32)[:, :, None, None], s.shape[:-1] + (1,))
    return jax.nn.softmax(jnp.concatenate([s, sk], axis=-1), axis=-1)[..., :-1]


def swa_prompt(q, k, v, sink):
    B, T = q.shape[0], q.shape[1]
    nc = T // CHUNK
    G = N_HEADS // N_KV
    qc = q.reshape(B, nc, CHUNK, N_KV, G, HEAD_DIM)

    def windows(z):
        pad = jnp.zeros((B, SWA_CHUNKS * CHUNK) + z.shape[2:], z.dtype)
        zc = jnp.concatenate([pad, z], axis=1).reshape((B, nc + SWA_CHUNKS, CHUNK) + z.shape[2:])
        return jnp.concatenate([zc[:, j:j + nc] for j in range(SWA_CHUNKS + 1)], axis=2)

    kw, vw = windows(k), windows(v)
    kpos = (jnp.arange(nc)[:, None] - SWA_CHUNKS) * CHUNK + jnp.arange((SWA_CHUNKS + 1) * CHUNK)[None, :]
    s = jnp.einsum('bcqkgd,bcskd->bckgqs', qc, kw).astype(F32) * HEAD_DIM ** -0.5
    s = jnp.where((kpos >= 0)[None, :, None, None, None, :], s, NEG_INF)
    p = sink_softmax(s, sink.reshape(N_KV, G))
    o = jnp.einsum('bckgqs,bcskd->bcqkgd', p.astype(v.dtype), vw)
    return o.reshape(B, T, A_Q)


def swa_sample(q, k, v, cache_k, cache_v, sink):
    B, T = q.shape[0], q.shape[1]
    G = N_HEADS // N_KV
    kk = jnp.concatenate([cache_k.astype(k.dtype), k], axis=1)
    vv = jnp.concatenate([cache_v.astype(v.dtype), v], axis=1)
    qg = q.reshape(B, T, N_KV, G, HEAD_DIM)
    s = jnp.einsum('bqkgd,bskd->bkgqs', qg, kk).astype(F32) * HEAD_DIM ** -0.5
    p = sink_softmax(s, sink.reshape(N_KV, G))
    o = jnp.einsum('bkgqs,bskd->bqkgd', p.astype(vv.dtype), vv)
    rows = cache_k.shape[1]
    return o.reshape(B, T, A_Q), kk[:, -rows:], vv[:, -rows:]


def gla_block(q, k, v, lg, s0):
    L = q.shape[1]
    b = jnp.cumsum(lg, axis=1)
    qe = q * jnp.exp(b)
    ke = k * jnp.exp(-b)
    mask = jnp.tril(jnp.ones((L, L), dtype=bool))
    att = jnp.where(mask[None, None], jnp.einsum('blhk,bshk->bhls', qe, ke), 0.0)
    o = jnp.einsum('blhk,bhkv->blhv', qe, s0) + jnp.einsum('bhls,bshv->blhv', att, v)
    b_last = b[:, -1]
    s_new = jnp.exp(b_last)[..., None] * s0 + jnp.einsum('bshk,bshv->bhkv', k * jnp.exp(b_last[:, None] - b), v)
    return o, s_new


def gla_chunked(q, k, v, lg, s0, blk):
    B, T, H = q.shape[0], q.shape[1], q.shape[2]
    nb = T // blk

    def to_blocks(z):
        return jnp.moveaxis(z.reshape(B, nb, blk, H, z.shape[-1]), 1, 0)

    def step(s, blk_in):
        qb, kb, vb, gb = blk_in
        o, s = gla_block(qb, kb, vb, gb, s)
        return s, o

    s, o = lax.scan(step, s0, (to_blocks(q), to_blocks(k), to_blocks(v), to_blocks(lg)))
    return jnp.moveaxis(o, 0, 1).reshape(B, T, H, v.shape[-1]), s


def causal_conv(x, prev, w, b):
    T = x.shape[1]
    xp = jnp.concatenate([prev.astype(x.dtype), x], axis=1)
    y = b
    for j in range(CONV_W):
        y = y + xp[:, j:j + T] * w[j]
    return y, xp[:, xp.shape[1] - (CONV_W - 1):]


def rg_lru(xc, pos, h0, w_a, b_a, w_x, b_x, lam):
    B, T, W = xc.shape
    xf = xc.astype(F32)
    xb = xf.reshape(B, T, LRU_BLOCKS, LRU_BW)
    r = jax.nn.sigmoid(jnp.einsum('btnc,ncd->btnd', xb, w_a.astype(F32)).reshape(B, T, W) + b_a)
    i = jax.nn.sigmoid(jnp.einsum('btnc,ncd->btnd', xb, w_x.astype(F32)).reshape(B, T, W) + b_x)
    log_a = -LRU_C * r * jax.nn.softplus(-lam.astype(F32))
    a = jnp.exp(log_a)
    mult = jnp.sqrt(-jnp.expm1(2.0 * log_a))
    reset = (pos == 0)[None, :, None]
    mult = jnp.where(reset, 1.0, mult)
    a = jnp.where(reset, 0.0, a)
    u = mult * i * xf
    u = u.at[:, 0].add(a[:, 0] * h0.astype(F32))

    def comb(left, right):
        a1, b1 = left
        a2, b2 = right
        return a1 * a2, a2 * b1 + b2

    _, h = lax.associative_scan(comb, (a, u), axis=1)
    return h, h[:, -1]


def swiglu(x, wg, wu, wd):
    return (jax.nn.silu(x @ wg) * (x @ wu)) @ wd


def moe_ffn(x, w_router, wg, wu, wd):
    logits = (x @ w_router).astype(F32)
    top_v, top_i = lax.top_k(logits, TOP_K)
    wts = jax.nn.softmax(top_v, axis=-1)
    combine = jnp.sum(jax.nn.one_hot(top_i, N_EXPERTS, dtype=F32) * wts[..., None], axis=-2).astype(x.dtype)
    y = jnp.zeros_like(x)
    for e in range(N_EXPERTS):
        y = y + combine[..., e:e + 1] * swiglu(x, wg[e], wu[e], wd[e])
    return y


def token_mix(h, pos, kv_cache, gla_s0, conv_prev, lru_h0, w_in, sink, w_gla_lr2, b_gla_gate, g_gla_norm,
              w_conv, b_conv, w_lru_a, b_lru_a, w_lru_x, b_lru_x, lru_lambda, w_branch, w_gate, b_gate, w_out):
    B, T, _ = h.shape
    u = h @ w_in
    aq, ak, av, gq, gk, gv, gr, glr, cx, cy = jnp.split(u, SPLIT_POINTS, axis=-1)
    q = rope(aq.reshape(B, T, N_HEADS, HEAD_DIM), pos)
    k = rope(ak.reshape(B, T, N_KV, HEAD_DIM), pos)
    v = av.reshape(B, T, N_KV, HEAD_DIM)
    if kv_cache is None:
        oa = swa_prompt(q, k, v, sink)
        new_k, new_v = k[:, T - WINDOW:], v[:, T - WINDOW:]
        blk = CHUNK
    else:
        oa, new_k, new_v = swa_sample(q, k, v, kv_cache[0], kv_cache[1], sink)
        blk = T
    gq_ = gq.reshape(B, T, GLA_HEADS, GLA_DK).astype(F32) * GLA_DK ** -0.5
    gk_ = gk.reshape(B, T, GLA_HEADS, GLA_DK).astype(F32)
    gv_ = gv.reshape(B, T, GLA_HEADS, GLA_DV).astype(F32)
    lg = (jax.nn.log_sigmoid((glr @ w_gla_lr2 + b_gla_gate).astype(F32)) / GLA_TAU).reshape(B, T, GLA_HEADS, GLA_DK)
    og, s_gla = gla_chunked(gq_, gk_, gv_, lg, gla_s0.astype(F32), blk)
    ob = rms_norm(og, g_gla_norm) * jax.nn.silu(gr.reshape(B, T, GLA_HEADS, GLA_DV).astype(F32))
    ob = ob.reshape(B, T, G_V).astype(h.dtype)
    xc, conv_state = causal_conv(cx, conv_prev, w_conv, b_conv)
    hl, h_last = rg_lru(xc, pos, lru_h0, w_lru_a, b_lru_a, w_lru_x, b_lru_x, lru_lambda)
    oc = (jax.nn.gelu(cy.astype(F32)) * hl).astype(h.dtype)
    branches = jnp.stack([oa.astype(h.dtype), ob, oc], axis=-2)
    proj = jnp.einsum('btnc,ncd->btnd', branches, w_branch)
    gates = jax.nn.sigmoid((h @ w_gate + b_gate).astype(F32)).reshape(B, T, N_BRANCH, D_MODEL)
    merged = jnp.sum(gates * proj, axis=2).astype(h.dtype)
    y = merged @ w_out
    return y, (new_k, new_v, s_gla.astype(h.dtype), conv_state, h_last.astype(h.dtype))


def setup_inputs(seed: int = 0) -> dict:
    key = jax.random.key(seed)
    ks = iter(jax.random.split(key, 40))
    n_dense = (DEPTH + 1) // 2
    n_moe = DEPTH // 2
    rows = min(WINDOW, PAST_LEN)

    def nrm(shape, scale):
        return jax.random.normal(next(ks), shape, F32) * scale

    def gain(shape):
        return 1.0 + nrm(shape, 0.02)

    u = jax.random.uniform(next(ks), (DEPTH, LRU_WIDTH), F32, 0.9, 0.999)
    a = u ** (1.0 / LRU_C)
    lru_lambda = jnp.log(a) - jnp.log1p(-a)
    return {
        "x_prompt": nrm((BATCH, SEQ, D_MODEL), 1.0),
        "x_sample": nrm((DEC_BATCH, DEC_SEQ, D_MODEL), 1.0),
        "cache_swa_k": nrm((DEPTH, DEC_BATCH, rows, N_KV, HEAD_DIM), 1.0),
        "cache_swa_v": nrm((DEPTH, DEC_BATCH, rows, N_KV, HEAD_DIM), 1.0),
        "state_gla": nrm((DEPTH, DEC_BATCH, GLA_HEADS, GLA_DK, GLA_DV), 1.0),
        "state_conv": nrm((DEPTH, DEC_BATCH, CONV_W - 1, LRU_WIDTH), 1.0),
        "state_lru": nrm((DEPTH, DEC_BATCH, LRU_WIDTH), 0.5),
        "ln_mix": gain((DEPTH, D_MODEL)),
        "w_in": nrm((DEPTH, D_MODEL, IN_COLS), D_MODEL ** -0.5),
        "attn_sink": nrm((DEPTH, N_HEADS), 1.0),
        "w_gla_lr2": nrm((DEPTH, GLA_LOWRANK, G_QK), GLA_LOWRANK ** -0.5),
        "b_gla_gate": nrm((DEPTH, G_QK), 0.1),
        "g_gla_norm": gain((DEPTH, GLA_DV)),
        "w_conv": nrm((DEPTH, CONV_W, LRU_WIDTH), CONV_W ** -0.5),
        "b_conv": nrm((DEPTH, LRU_WIDTH), 0.02),
        "w_lru_a": nrm((DEPTH, LRU_BLOCKS, LRU_BW, LRU_BW), LRU_BW ** -0.5),
        "b_lru_a": nrm((DEPTH, LRU_WIDTH), 0.02),
        "w_lru_x": nrm((DEPTH, LRU_BLOCKS, LRU_BW, LRU_BW), LRU_BW ** -0.5),
        "b_lru_x": nrm((DEPTH, LRU_WIDTH), 0.02),
        "lru_lambda": lru_lambda,
        "w_branch": nrm((DEPTH, N_BRANCH, BRANCH_W, D_MODEL), BRANCH_W ** -0.5),
        "w_gate": nrm((DEPTH, D_MODEL, N_BRANCH * D_MODEL), D_MODEL ** -0.5),
        "b_gate": nrm((DEPTH, N_BRANCH * D_MODEL), 0.02),
        "w_out": nrm((DEPTH, D_MODEL, D_MODEL), D_MODEL ** -0.5),
        "ln_ffn": gain((DEPTH, D_MODEL)),
        "w_ff_gate": nrm((n_dense, D_MODEL, FFN_DIM), D_MODEL ** -0.5),
        "w_ff_up": nrm((n_dense, D_MODEL, FFN_DIM), D_MODEL ** -0.5),
        "w_ff_down": nrm((n_dense, FFN_DIM, D_MODEL), FFN_DIM ** -0.5),
        "w_router": nrm((n_moe, D_MODEL, N_EXPERTS), D_MODEL ** -0.5),
        "w_moe_gate": nrm((n_moe, N_EXPERTS, D_MODEL, FFN_DIM), D_MODEL ** -0.5),
        "w_moe_up": nrm((n_moe, N_EXPERTS, D_MODEL, FFN_DIM), D_MODEL ** -0.5),
        "w_moe_down": nrm((n_moe, N_EXPERTS, FFN_DIM, D_MODEL), FFN_DIM ** -0.5),
        "ln_final": gain((D_MODEL,)),
    }


def reference(x_prompt, x_sample, cache_swa_k, cache_swa_v, state_gla, state_conv, state_lru, ln_mix, w_in,
              attn_sink, w_gla_lr2, b_gla_gate, g_gla_norm, w_conv, b_conv, w_lru_a, b_lru_a, w_lru_x, b_lru_x,
              lru_lambda, w_branch, w_gate, b_gate, w_out, ln_ffn, w_ff_gate, w_ff_up, w_ff_down, w_router,
              w_moe_gate, w_moe_up, w_moe_down, ln_final):
    hp, hs = x_prompt, x_sample
    bp = x_prompt.shape[0]
    pos_p = jnp.arange(x_prompt.shape[1])
    pos_s = PAST_LEN + jnp.arange(x_sample.shape[1])
    pk, pv, pg, pc, pl = [], [], [], [], []
    sk, sv, sg, sc, sl = [], [], [], [], []
    for l in range(DEPTH):
        mix_w = (w_in[l], attn_sink[l], w_gla_lr2[l], b_gla_gate[l], g_gla_norm[l], w_conv[l], b_conv[l],
                 w_lru_a[l], b_lru_a[l], w_lru_x[l], b_lru_x[l], lru_lambda[l], w_branch[l], w_gate[l],
                 b_gate[l], w_out[l])
        yp, st_p = token_mix(rms_norm(hp, ln_mix[l]), pos_p, None,
                             jnp.zeros((bp, GLA_HEADS, GLA_DK, GLA_DV), F32),
                             jnp.zeros((bp, CONV_W - 1, LRU_WIDTH), hp.dtype),
                             jnp.zeros((bp, LRU_WIDTH), F32), *mix_w)
        ys, st_s = token_mix(rms_norm(hs, ln_mix[l]), pos_s, (cache_swa_k[l], cache_swa_v[l]),
                             state_gla[l], state_conv[l], state_lru[l], *mix_w)
        hp = hp + yp
        hs = hs + ys
        np_ = rms_norm(hp, ln_ffn[l])
        ns_ = rms_norm(hs, ln_ffn[l])
        if l % 2 == 0:
            i = l // 2
            hp = hp + swiglu(np_, w_ff_gate[i], w_ff_up[i], w_ff_down[i])
            hs = hs + swiglu(ns_, w_ff_gate[i], w_ff_up[i], w_ff_down[i])
        else:
            i = l // 2
            hp = hp + moe_ffn(np_, w_router[i], w_moe_gate[i], w_moe_up[i], w_moe_down[i])
            hs = hs + moe_ffn(ns_, w_router[i], w_moe_gate[i], w_moe_up[i], w_moe_down[i])
        pk.append(st_p[0]); pv.append(st_p[1]); pg.append(st_p[2]); pc.append(st_p[3]); pl.append(st_p[4])
        sk.append(st_s[0]); sv.append(st_s[1]); sg.append(st_s[2]); sc.append(st_s[3]); sl.append(st_s[4])
    y_prompt = rms_norm(hp, ln_final)
    y_sample = rms_norm(hs, ln_final)
    return (y_prompt, y_sample,
            jnp.stack(pk), jnp.stack(pv), jnp.stack(pg), jnp.stack(pc), jnp.stack(pl),
            jnp.stack(sk), jnp.stack(sv), jnp.stack(sg), jnp.stack(sc), jnp.stack(sl))
```

```python
import functools

import jax
import jax.numpy as jnp
from jax import lax
from jax.experimental import pallas as pl
from jax.experimental.pallas import tpu as pltpu

F32 = jnp.float32
BF16 = jnp.bfloat16

CHUNK = 64
PAST_LEN = 2048
TOP_K = 2
EPS = 1e-6
NEG_INF = -1e30
ROPE_THETA = 10000.0
GLA_TAU = 16.0
LRU_C = 8.0

LANES = 128
SUBLANES = 8
V7X_VMEM_BYTES = 64 * 1024 * 1024
MIB = 1024 * 1024
FFN_ROW_TILE = 768
FFN_DOWN_COLS = 1024


def _params(semantics, vmem_mib):
    assert vmem_mib * MIB < V7X_VMEM_BYTES
    return pltpu.CompilerParams(dimension_semantics=semantics, vmem_limit_bytes=vmem_mib * MIB)


def _tile(n, target):
    t = 1
    while t * 2 <= target and n % (t * 2) == 0:
        t *= 2
    return t


def _rope_table_kernel(cos_ref, sin_ref):
    rows, hd = cos_ref.shape
    half = hd // 2
    lane = lax.broadcasted_iota(jnp.int32, (rows, hd), 1)
    j = jnp.where(lane < half, lane, lane - half).astype(F32)
    inv = jnp.power(jnp.float32(ROPE_THETA), -j / half)
    pos = (lax.broadcasted_iota(jnp.int32, (rows, hd), 0) + pl.program_id(0) * rows).astype(F32)
    ang = pos * inv
    cos_ref[...] = jnp.cos(ang)
    s = jnp.sin(ang)
    sin_ref[...] = jnp.where(lane < half, -s, s)


def _rope_tables(n_pos, head_dim):
    rows = _tile(n_pos, 512)
    return pl.pallas_call(
        _rope_table_kernel,
        grid=(n_pos // rows,),
        out_specs=[pl.BlockSpec((rows, head_dim), lambda i: (i, 0))] * 2,
        out_shape=[jax.ShapeDtypeStruct((n_pos, head_dim), F32)] * 2,
        compiler_params=_params(("arbitrary",), 16),
    )()


def _rms(xf, g):
    return xf * lax.rsqrt(jnp.mean(xf * xf, axis=-1, keepdims=True) + EPS) * g


def _addnorm_kernel(*refs, has_y, emit_x):
    it = iter(refs)
    x_ref = next(it)
    y_ref = next(it) if has_y else None
    g_ref = next(it)
    xo_ref = next(it) if emit_x else None
    n_ref = next(it)
    x = x_ref[...]
    if has_y:
        x = x + y_ref[...]
    if emit_x:
        xo_ref[...] = x
    n_ref[...] = _rms(x, g_ref[...]).astype(n_ref.dtype)


def _addnorm(x, y, g, out_dtype):
    m, d = x.shape
    tm = _tile(m, 256)
    has_y = y is not None
    row = pl.BlockSpec((tm, d), lambda i: (i, 0))
    ins = [x] + ([y] if has_y else []) + [g.reshape(1, d)]
    in_specs = [row] * (2 if has_y else 1) + [pl.BlockSpec((1, d), lambda i: (0, 0))]
    out_shape = ([jax.ShapeDtypeStruct((m, d), F32)] if has_y else []) + [jax.ShapeDtypeStruct((m, d), out_dtype)]
    outs = pl.pallas_call(
        functools.partial(_addnorm_kernel, has_y=has_y, emit_x=has_y),
        grid=(m // tm,),
        in_specs=in_specs,
        out_specs=[row] * len(out_shape),
        out_shape=out_shape,
        compiler_params=_params(("arbitrary",), 48),
    )(*ins)
    return (outs[0], outs[1]) if has_y else (x, outs[0])


def _mm_kernel(x_ref, w_ref, o_ref, wbf_ref):
    @pl.when(pl.program_id(1) == 0)
    def _():
        wbf_ref[...] = w_ref[...].astype(BF16)

    o_ref[...] = jnp.dot(x_ref[...], wbf_ref[...], preferred_element_type=F32).astype(o_ref.dtype)


def _mm(x, w, layer, col0, n, out_dtype=F32):
    m, k = x.shape
    tm = _tile(m, 1024)
    tn = _tile(n, 512)
    assert col0 % tn == 0 and tn % LANES == 0
    off = col0 // tn
    return pl.pallas_call(
        _mm_kernel,
        grid=(n // tn, m // tm),
        in_specs=[
            pl.BlockSpec((tm, k), lambda j, i: (i, 0)),
            pl.BlockSpec((None, k, tn), lambda j, i: (layer, 0, j + off)),
        ],
        out_specs=pl.BlockSpec((tm, tn), lambda j, i: (i, j)),
        out_shape=jax.ShapeDtypeStruct((m, n), out_dtype),
        scratch_shapes=[pltpu.VMEM((k, tn), BF16)],
        compiler_params=_params(("arbitrary", "arbitrary"), 56),
    )(x, w)


def _log_sigmoid(z):
    y = -z
    return -(jnp.maximum(y, 0.0) + jnp.log1p(jnp.exp(-jnp.abs(y))))


def _lowrank_kernel(n_ref, w1_ref, w2_ref, b_ref, o_ref):
    glr = jnp.dot(n_ref[...], w1_ref[...], preferred_element_type=F32)
    z = jnp.dot(glr.astype(BF16), w2_ref[...], preferred_element_type=F32) + b_ref[...]
    o_ref[...] = _log_sigmoid(z) / GLA_TAU


def _lowrank_gate(n, w1, w2, b):
    m, d = n.shape
    gqk = w2.shape[1]
    tm = _tile(m, 512)
    return pl.pallas_call(
        _lowrank_kernel,
        grid=(m // tm,),
        in_specs=[
            pl.BlockSpec((tm, d), lambda i: (i, 0)),
            pl.BlockSpec(w1.shape, lambda i: (0, 0)),
            pl.BlockSpec(w2.shape, lambda i: (0, 0)),
            pl.BlockSpec((1, gqk), lambda i: (0, 0)),
        ],
        out_specs=pl.BlockSpec((tm, gqk), lambda i: (i, 0)),
        out_shape=jax.ShapeDtypeStruct((m, gqk), F32),
        compiler_params=_params(("arbitrary",), 32),
    )(n, w1, w2, b)


def _rope(x, cos, sin):
    return x * cos + pltpu.roll(x, x.shape[-1] // 2, 1) * sin


def _swa_kernel(sink_ref, q_ref, k0_ref, k1_ref, k2_ref, v0_ref, v1_ref, v2_ref,
                c0_ref, c1_ref, c2_ref, s0_ref, s1_ref, s2_ref, o_ref, krot_ref, *, groups, cached):
    kvh = pl.program_id(1)
    c = pl.program_id(2)
    hd = k2_ref.shape[-1]
    cosq, sinq = c2_ref[...], s2_ref[...]
    k2 = _rope(k2_ref[...], cosq, sinq)
    krot_ref[...] = k2
    if cached:
        k0, k1 = k0_ref[...], k1_ref[...]
    else:
        k0 = _rope(k0_ref[...], c0_ref[...], s0_ref[...])
        k1 = _rope(k1_ref[...], c1_ref[...], s1_ref[...])
    kall = jnp.concatenate([k0, k1, k2], axis=0).astype(BF16)
    vall = jnp.concatenate([v0_ref[...], v1_ref[...], v2_ref[...]], axis=0).astype(BF16)
    qall = jnp.concatenate(
        [_rope(q_ref[:, g * hd:(g + 1) * hd], cosq, sinq) for g in range(groups)], axis=0).astype(BF16)
    s = lax.dot_general(qall, kall, (((1,), (1,)), ((), ())), preferred_element_type=F32) * hd ** -0.5
    if not cached:
        col = lax.broadcasted_iota(jnp.int32, s.shape, 1)
        s = jnp.where(col >= jnp.maximum(2 - c, 0) * CHUNK, s, NEG_INF)
    sink = jnp.concatenate(
        [jnp.full((CHUNK, 1), sink_ref[kvh * groups + g], F32) for g in range(groups)], axis=0)
    mx = jnp.maximum(jnp.max(s, axis=-1, keepdims=True), sink)
    p = jnp.exp(s - mx)
    den = jnp.sum(p, axis=-1, keepdims=True) + jnp.exp(sink - mx)
    p = p / den
    o = jnp.dot(p.astype(BF16), vall, preferred_element_type=F32)
    for g in range(groups):
        o_ref[:, g * hd:(g + 1) * hd] = o[g * CHUNK:(g + 1) * CHUNK].astype(o_ref.dtype)


def _swa(u, sink, cos, sin, cache_k, cache_v, *, n_seq, n_chunks, row0, n_kv, groups, hd, pos0, m_rows, aq):
    cached = cache_k is not None
    akv = n_kv * hd
    rb0 = row0 // CHUNK
    pb0 = pos0 // CHUNK
    qw = groups * hd

    def rowblk(b, c):
        return rb0 + b * n_chunks + c

    def prev(j):
        return lambda b, h, c: (rowblk(b, jnp.maximum(c - 2 + j, 0)), 0)

    q_spec = pl.BlockSpec((CHUNK, qw), lambda b, h, c: (rowblk(b, c), h))
    kcol, vcol = aq // hd, (aq + akv) // hd
    if cached:
        kprev = [pl.BlockSpec((CHUNK, hd), lambda b, h, c, j=j: (2 * b + j, h)) for j in range(2)]
        vprev = kprev
        kin = [cache_k, cache_k]
        vin = [cache_v, cache_v]
    else:
        kprev = [pl.BlockSpec((CHUNK, hd), lambda b, h, c, j=j: (rowblk(b, jnp.maximum(c - 2 + j, 0)), kcol + h))
                 for j in range(2)]
        vprev = [pl.BlockSpec((CHUNK, hd), lambda b, h, c, j=j: (rowblk(b, jnp.maximum(c - 2 + j, 0)), vcol + h))
                 for j in range(2)]
        kin = [u, u]
        vin = [u, u]
    kcur = pl.BlockSpec((CHUNK, hd), lambda b, h, c: (rowblk(b, c), kcol + h))
    vcur = pl.BlockSpec((CHUNK, hd), lambda b, h, c: (rowblk(b, c), vcol + h))
    tab = [pl.BlockSpec((CHUNK, hd), lambda b, h, c, j=j: (pb0 + jnp.maximum(c - 2 + j, 0), 0)) for j in range(3)]
    out_rows = n_seq * n_chunks * CHUNK
    o_spec = pl.BlockSpec((CHUNK, qw), lambda b, h, c: (b * n_chunks + c, h))
    kr_spec = pl.BlockSpec((CHUNK, hd), lambda b, h, c: (b * n_chunks + c, h))
    return pl.pallas_call(
        functools.partial(_swa_kernel, groups=groups, cached=cached),
        grid=(n_seq, n_kv, n_chunks),
        in_specs=[pl.BlockSpec(memory_space=pltpu.SMEM), q_spec] + kprev + [kcur] + vprev + [vcur] + tab + tab,
        out_specs=[o_spec, kr_spec],
        out_shape=[jax.ShapeDtypeStruct((out_rows, aq), BF16), jax.ShapeDtypeStruct((out_rows, akv), F32)],
        compiler_params=_params(("arbitrary",) * 3, 32),
    )(sink, u, *kin, u, *vin, u, cos, cos, cos, sin, sin, sin)


def _cumsum_rows(x):
    row = lax.broadcasted_iota(jnp.int32, x.shape, 0)
    d = 1
    while d < x.shape[0]:
        x = x + jnp.where(row >= d, pltpu.roll(x, d, 0), 0.0)
        d *= 2
    return x


def _gla_kernel(*refs, has_init):
    it = iter(refs)
    q_ref, k_ref, v_ref, r_ref, lg_ref, g_ref = (next(it) for _ in range(6))
    s0_ref = next(it) if has_init else None
    o_ref, sout_ref, st_ref = next(it), next(it), next(it)
    c = pl.program_id(2)
    dk = q_ref.shape[-1]

    @pl.when(c == 0)
    def _():
        if has_init:
            st_ref[...] = s0_ref[...].T
        else:
            st_ref[...] = jnp.zeros_like(st_ref)

    b = _cumsum_rows(lg_ref[...])
    b_last = b[CHUNK - 1:CHUNK, :]
    k = k_ref[...]
    qe = (q_ref[...] * dk ** -0.5 * jnp.exp(b)).astype(BF16)
    ke = (k * jnp.exp(-b)).astype(BF16)
    kd = (k * jnp.exp(b_last - b)).astype(BF16)
    v = v_ref[...].astype(BF16)
    att = lax.dot_general(qe, ke, (((1,), (1,)), ((), ())), preferred_element_type=F32)
    tri = lax.broadcasted_iota(jnp.int32, att.shape, 0) >= lax.broadcasted_iota(jnp.int32, att.shape, 1)
    att = jnp.where(tri, att, 0.0)
    st = st_ref[...]
    o = lax.dot_general(qe, st.astype(BF16), (((1,), (1,)), ((), ())), preferred_element_type=F32)
    o = o + jnp.dot(att.astype(BF16), v, preferred_element_type=F32)
    st_new = jnp.exp(b_last) * st + lax.dot_general(v, kd, (((0,), (0,)), ((), ())), preferred_element_type=F32)
    st_ref[...] = st_new
    r = r_ref[...]
    o_ref[...] = (_rms(o, g_ref[...]) * (r * jax.nn.sigmoid(r))).astype(o_ref.dtype)

    @pl.when(c == pl.num_programs(2) - 1)
    def _():
        sout_ref[...] = st_new.T


def _gla(u, lg, g_norm, s0, *, n_seq, n_chunks, row0, heads, dk, dv, q_col, k_col, v_col, r_col):
    has_init = s0 is not None
    rb0 = row0 // CHUNK

    def col(width, col0):
        assert col0 % width == 0
        return pl.BlockSpec((CHUNK, width), lambda b, h, c: (rb0 + b * n_chunks + c, col0 // width + h))

    in_specs = [col(dk, q_col), col(dk, k_col), col(dv, v_col), col(dv, r_col),
                pl.BlockSpec((CHUNK, dk), lambda b, h, c: (rb0 + b * n_chunks + c, h)),
                pl.BlockSpec((1, dv), lambda b, h, c: (0, 0))]
    ins = [u, u, u, u, lg, g_norm.reshape(1, dv)]
    if has_init:
        in_specs.append(pl.BlockSpec((None, None, dk, dv), lambda b, h, c: (b, h, 0, 0)))
        ins.append(s0)
    out_rows = n_seq * n_chunks * CHUNK
    return pl.pallas_call(
        functools.partial(_gla_kernel, has_init=has_init),
        grid=(n_seq, heads, n_chunks),
        in_specs=in_specs,
        out_specs=[pl.BlockSpec((CHUNK, dv), lambda b, h, c: (b * n_chunks + c, h)),
                   pl.BlockSpec((None, None, dk, dv), lambda b, h, c: (b, h, 0, 0))],
        out_shape=[jax.ShapeDtypeStruct((out_rows, heads * dv), BF16),
                   jax.ShapeDtypeStruct((n_seq, heads, dk, dv), F32)],
        scratch_shapes=[pltpu.VMEM((dv, dk), F32)],
        compiler_params=_params(("arbitrary",) * 3, 32),
    )(*ins)


def _softplus(y):
    return jnp.maximum(y, 0.0) + jnp.log1p(jnp.exp(-jnp.abs(y)))


def _lru_kernel(x_ref, xprev_ref, cprev_ref, y_ref, wc_ref, bc_ref, wa_ref, ba_ref, wx_ref, bx_ref, lam_ref, h0_ref,
                o_ref, hl_ref, h_ref, a_ref, u_ref, *, stream_start, conv_w):
    c = pl.program_id(1)
    t, w = x_ref.shape
    nblk, bw = wa_ref.shape[0], wa_ref.shape[1]

    @pl.when(c == 0)
    def _():
        h_ref[...] = h0_ref[...]

    prev = jnp.where(c == 0, cprev_ref[...], xprev_ref[...])
    ext = jnp.concatenate([prev, x_ref[...]], axis=0)
    xc = bc_ref[...]
    for j in range(conv_w):
        shift = conv_w - 1 - j
        xs = ext if shift == 0 else pltpu.roll(ext, shift, 0)
        xc = xc + xs[SUBLANES:] * wc_ref[j:j + 1, :]
    xb = xc.astype(BF16)
    ra = jnp.concatenate([jnp.dot(xb[:, n * bw:(n + 1) * bw], wa_ref[n], preferred_element_type=F32)
                          for n in range(nblk)], axis=1)
    ia = jnp.concatenate([jnp.dot(xb[:, n * bw:(n + 1) * bw], wx_ref[n], preferred_element_type=F32)
                          for n in range(nblk)], axis=1)
    r = jax.nn.sigmoid(ra + ba_ref[...])
    gate = jax.nn.sigmoid(ia + bx_ref[...])
    log_a = -LRU_C * r * _softplus(-lam_ref[...])
    a = jnp.exp(log_a)
    mult = jnp.sqrt(jnp.tanh(-log_a) * (a * a + 1.0))
    if stream_start:
        first = (lax.broadcasted_iota(jnp.int32, (t, w), 0) == 0) & (c == 0)
        mult = jnp.where(first, 1.0, mult)
        a = jnp.where(first, 0.0, a)
    a_ref[...] = a
    u_ref[...] = mult * gate * xc

    def step(i, h):
        h = a_ref[pl.ds(i, 1), :] * h + u_ref[pl.ds(i, 1), :]
        u_ref[pl.ds(i, 1), :] = h
        return h

    h = lax.fori_loop(0, t, step, h_ref[...], unroll=8)
    h_ref[...] = h
    hl_ref[...] = h
    o_ref[...] = (jax.nn.gelu(y_ref[...]) * u_ref[...]).astype(o_ref.dtype)


def _lru(uc, cprev8, h0, wc, bc, wa, ba, wx, bx, lam, *, n_seq, n_chunks, row0, width, stream_start):
    rb0 = row0 // CHUNK
    per = CHUNK // SUBLANES
    conv_w = wc.shape[0]
    vec = pl.BlockSpec((1, width), lambda b, c: (0, 0))
    blk = pl.BlockSpec(wa.shape, lambda b, c: (0, 0, 0))
    out_rows = n_seq * n_chunks * CHUNK
    return pl.pallas_call(
        functools.partial(_lru_kernel, stream_start=stream_start, conv_w=conv_w),
        grid=(n_seq, n_chunks),
        in_specs=[
            pl.BlockSpec((CHUNK, width), lambda b, c: (rb0 + b * n_chunks + c, 0)),
            pl.BlockSpec((SUBLANES, width), lambda b, c: (jnp.maximum((rb0 + b * n_chunks + c) * per - 1, 0), 0)),
            pl.BlockSpec((None, SUBLANES, width), lambda b, c: (b, 0, 0)),
            pl.BlockSpec((CHUNK, width), lambda b, c: (rb0 + b * n_chunks + c, 1)),
            pl.BlockSpec(wc.shape, lambda b, c: (0, 0)), vec, blk, vec, blk, vec, vec,
            pl.BlockSpec((None, 1, width), lambda b, c: (b, 0, 0)),
        ],
        out_specs=[pl.BlockSpec((CHUNK, width), lambda b, c: (b * n_chunks + c, 0)),
                   pl.BlockSpec((None, 1, width), lambda b, c: (b, 0, 0))],
        out_shape=[jax.ShapeDtypeStruct((out_rows, width), BF16), jax.ShapeDtypeStruct((n_seq, 1, width), F32)],
        scratch_shapes=[pltpu.VMEM((1, width), F32), pltpu.VMEM((CHUNK, width), F32), pltpu.VMEM((CHUNK, width), F32)],
        compiler_params=_params(("arbitrary",) * 2, 32),
    )(uc, uc, cprev8, uc, wc, bc, wa, ba, wx, bx, lam, h0)


def _merge_kernel(n_ref, a_ref, b_ref, c_ref, wg0_ref, wg1_ref, wg2_ref, wb_ref, bg0_ref, bg1_ref, bg2_ref,
                  o_ref, wgbf_ref, wbbf_ref):
    @pl.when(pl.program_id(1) == 0)
    def _():
        for i, wg_ref in enumerate((wg0_ref, wg1_ref, wg2_ref)):
            wgbf_ref[i] = wg_ref[...].astype(BF16)
            wbbf_ref[i] = wb_ref[i].astype(BF16)

    n = n_ref[...]
    acc = None
    for i, (br_ref, bg_ref) in enumerate(((a_ref, bg0_ref), (b_ref, bg1_ref), (c_ref, bg2_ref))):
        gate = jax.nn.sigmoid(jnp.dot(n, wgbf_ref[i], preferred_element_type=F32) + bg_ref[...])
        term = gate * jnp.dot(br_ref[...], wbbf_ref[i], preferred_element_type=F32)
        acc = term if acc is None else acc + term
    o_ref[...] = acc.astype(o_ref.dtype)


def _merge(n, oa, ob, oc, w_gate, b_gate, w_branch, layer):
    m, d = n.shape
    bw = oa.shape[1]
    nb = w_branch.shape[1]
    assert nb == 3
    tm = _tile(m, 512)
    tn = _tile(d, 256)
    nj = d // tn
    single = pl.Buffered(1)
    wg_specs = [pl.BlockSpec((None, d, tn), lambda j, i, b=b: (layer, 0, b * nj + j), pipeline_mode=single)
                for b in range(nb)]
    bg_specs = [pl.BlockSpec((None, 1, tn), lambda j, i, b=b: (layer, 0, b * nj + j)) for b in range(nb)]
    br_spec = pl.BlockSpec((tm, bw), lambda j, i: (i, 0))
    return pl.pallas_call(
        _merge_kernel,
        grid=(nj, m // tm),
        in_specs=[pl.BlockSpec((tm, d), lambda j, i: (i, 0)), br_spec, br_spec, br_spec] + wg_specs
        + [pl.BlockSpec((None, nb, bw, tn), lambda j, i: (layer, 0, 0, j), pipeline_mode=single)] + bg_specs,
        out_specs=pl.BlockSpec((tm, tn), lambda j, i: (i, j)),
        out_shape=jax.ShapeDtypeStruct((m, d), BF16),
        scratch_shapes=[pltpu.VMEM((nb, d, tn), BF16), pltpu.VMEM((nb, bw, tn), BF16)],
        compiler_params=_params(("arbitrary", "arbitrary"), 56),
    )(n, oa, ob, oc, w_gate, w_gate, w_gate, w_branch, b_gate, b_gate, b_gate)


def _ffn_kernel(te_ref, nv_ref, x_ref, wg_ref, wu_ref, wd_ref, o_ref):
    i, j = pl.program_id(0), pl.program_id(1)

    @pl.when(i < nv_ref[0])
    def _():
        x = x_ref[...]
        g = jnp.dot(x, wg_ref[...].astype(BF16), preferred_element_type=F32)
        u = jnp.dot(x, wu_ref[...].astype(BF16), preferred_element_type=F32)
        h = (g * jax.nn.sigmoid(g) * u).astype(BF16)
        d = o_ref.shape[1]
        dc = min(d, FFN_DOWN_COLS)

        def down(n0):
            return jnp.dot(h, wd_ref[:, n0:n0 + dc].astype(BF16), preferred_element_type=F32)

        @pl.when(j == 0)
        def _():
            for n0 in range(0, d, dc):
                o_ref[:, n0:n0 + dc] = down(n0)

        @pl.when(j > 0)
        def _():
            for n0 in range(0, d, dc):
                o_ref[:, n0:n0 + dc] += down(n0)

    @pl.when((i >= nv_ref[0]) & (j == 0))
    def _():
        o_ref[...] = jnp.zeros_like(o_ref)


def _ffn(x, wg, wu, wd, tile_expert, n_valid, tm):
    r, d = x.shape
    f = wg.shape[2]
    tf = _tile(f, 256)
    nj = f // tf
    single = pl.Buffered(1)

    def row(i, j, te, nv):
        return (jnp.minimum(i, nv[0] - 1), 0)

    def fidx(i, j, nv):
        return jnp.where(i < nv[0], j, nj - 1)

    def up(i, j, te, nv):
        return (te[jnp.minimum(i, nv[0] - 1)], 0, fidx(i, j, nv))

    def down(i, j, te, nv):
        return (te[jnp.minimum(i, nv[0] - 1)], fidx(i, j, nv), 0)

    return pl.pallas_call(
        _ffn_kernel,
        grid_spec=pltpu.PrefetchScalarGridSpec(
            num_scalar_prefetch=2,
            grid=(r // tm, nj),
            in_specs=[pl.BlockSpec((tm, d), row, pipeline_mode=single),
                      pl.BlockSpec((None, d, tf), up), pl.BlockSpec((None, d, tf), up),
                      pl.BlockSpec((None, tf, d), down)],
            out_specs=pl.BlockSpec((tm, d), lambda i, j, te, nv: (i, 0), pipeline_mode=single),
        ),
        out_shape=jax.ShapeDtypeStruct((r, d), F32),
        compiler_params=_params(("arbitrary", "arbitrary"), 60),
    )(tile_expert, n_valid, x, wg, wu, wd)


def _router_kernel(x_ref, g_ref, w_ref, meta_ref, cnt_ref, carry_ref, *, n_experts):
    @pl.when(pl.program_id(0) == 0)
    def _():
        carry_ref[...] = jnp.zeros_like(carry_ref)

    tm = x_ref.shape[0]
    n = _rms(x_ref[...], g_ref[...])
    logits = jnp.dot(n, w_ref[...], preferred_element_type=F32, precision=lax.Precision.HIGHEST)
    lane = lax.broadcasted_iota(jnp.int32, logits.shape, 1).astype(F32)
    logits = jnp.where(lane < n_experts, logits, -jnp.inf)
    m1 = jnp.max(logits, axis=-1, keepdims=True)
    i1 = jnp.min(jnp.where(logits == m1, lane, float(LANES)), axis=-1, keepdims=True)
    rest = jnp.where(lane == i1, -jnp.inf, logits)
    m2 = jnp.max(rest, axis=-1, keepdims=True)
    i2 = jnp.min(jnp.where(rest == m2, lane, float(LANES)), axis=-1, keepdims=True)
    e2 = jnp.exp(m2 - m1)
    den = 1.0 + e2
    oh1 = lane == i1
    oh2 = lane == i2
    both = jnp.where(oh1 | oh2, 1.0, 0.0)
    rr = lax.broadcasted_iota(jnp.int32, (tm, tm), 0)
    cc = lax.broadcasted_iota(jnp.int32, (tm, tm), 1)
    below = jnp.where(rr > cc, 1.0, 0.0).astype(BF16)
    before = jnp.dot(below, both.astype(BF16), preferred_element_type=F32) + carry_ref[...]
    rank1 = jnp.sum(jnp.where(oh1, before, 0.0), axis=-1, keepdims=True)
    rank2 = jnp.sum(jnp.where(oh2, before, 0.0), axis=-1, keepdims=True)
    cols = (i1, i2, 1.0 / den, e2 / den, rank1, rank2)
    meta = jnp.zeros(logits.shape, F32)
    for idx, val in enumerate(cols):
        meta = jnp.where(lane == idx, val, meta)
    meta_ref[...] = meta
    total = carry_ref[...] + jnp.sum(both, axis=0, keepdims=True)
    carry_ref[...] = total
    cnt_ref[...] = total


def _router(x, g, w_pad, n_experts):
    m, d = x.shape
    tm = _tile(m, 256)
    return pl.pallas_call(
        functools.partial(_router_kernel, n_experts=n_experts),
        grid=(m // tm,),
        in_specs=[pl.BlockSpec((tm, d), lambda i: (i, 0)), pl.BlockSpec((1, d), lambda i: (0, 0)),
                  pl.BlockSpec(w_pad.shape, lambda i: (0, 0))],
        out_specs=[pl.BlockSpec((tm, LANES), lambda i: (i, 0)), pl.BlockSpec((1, LANES), lambda i: (0, 0))],
        out_shape=[jax.ShapeDtypeStruct((m, LANES), F32), jax.ShapeDtypeStruct((1, LANES), F32)],
        scratch_shapes=[pltpu.VMEM((1, LANES), F32)],
        compiler_params=_params(("arbitrary",), 40),
    )(x, g.reshape(1, d), w_pad)


def _dispatch_kernel(dest_ref, x_hbm, xs_in_hbm, xs_hbm, sem, *, rows):
    del xs_in_hbm
    base = pl.program_id(0) * rows

    def copy(r, k):
        return pltpu.make_async_copy(x_hbm.at[pl.ds(base + r, 1)], xs_hbm.at[pl.ds(dest_ref[0, TOP_K * r + k], 1)], sem)

    def start(r, _):
        for k in range(TOP_K):
            copy(r, k).start()
        return 0

    def wait(r, _):
        for k in range(TOP_K):
            copy(r, k).wait()
        return 0

    lax.fori_loop(0, rows, start, 0)
    lax.fori_loop(0, rows, wait, 0)


def _dispatch(x, dest, n_rows):
    m, d = x.shape
    rows = _tile(m, 256)
    dest3 = dest.reshape(m // rows, 1, rows * TOP_K)
    return pl.pallas_call(
        functools.partial(_dispatch_kernel, rows=rows),
        grid=(m // rows,),
        in_specs=[pl.BlockSpec((None, 1, rows * TOP_K), lambda i: (i, 0, 0), memory_space=pltpu.SMEM),
                  pl.BlockSpec(memory_space=pl.ANY), pl.BlockSpec(memory_space=pl.ANY)],
        out_specs=pl.BlockSpec(memory_space=pl.ANY),
        out_shape=jax.ShapeDtypeStruct((n_rows, d), x.dtype),
        scratch_shapes=[pltpu.SemaphoreType.DMA(())],
        input_output_aliases={2: 0},
        compiler_params=_params(("arbitrary",), 16),
    )(dest3, x, jnp.zeros((n_rows, d), x.dtype))


def _combine_kernel(dest_ref, x_ref, meta_ref, y_hbm, o_ref, ya_ref, yb_ref, sem, *, rows):
    bufs = (ya_ref, yb_ref)

    def copy(r, k):
        return pltpu.make_async_copy(y_hbm.at[pl.ds(dest_ref[0, TOP_K * r + k], 1)], bufs[k].at[pl.ds(r, 1)], sem)

    def start(r, _):
        for k in range(TOP_K):
            copy(r, k).start()
        return 0

    def wait(r, _):
        for k in range(TOP_K):
            copy(r, k).wait()
        return 0

    lax.fori_loop(0, rows, start, 0)
    lax.fori_loop(0, rows, wait, 0)
    meta = meta_ref[...]
    o_ref[...] = x_ref[...] + (meta[:, 2:3] * ya_ref[...] + meta[:, 3:4] * yb_ref[...])


def _combine(x, meta, y, dest):
    m, d = x.shape
    rows = _tile(m, 128)
    dest3 = dest.reshape(m // rows, 1, rows * TOP_K)
    return pl.pallas_call(
        functools.partial(_combine_kernel, rows=rows),
        grid=(m // rows,),
        in_specs=[pl.BlockSpec((None, 1, rows * TOP_K), lambda i: (i, 0, 0), memory_space=pltpu.SMEM),
                  pl.BlockSpec((rows, d), lambda i: (i, 0)), pl.BlockSpec((rows, LANES), lambda i: (i, 0)),
                  pl.BlockSpec(memory_space=pl.ANY)],
        out_specs=pl.BlockSpec((rows, d), lambda i: (i, 0)),
        out_shape=jax.ShapeDtypeStruct((m, d), F32),
        scratch_shapes=[pltpu.VMEM((rows, d), F32), pltpu.VMEM((rows, d), F32), pltpu.SemaphoreType.DMA(())],
        compiler_params=_params(("arbitrary",), 32),
    )(dest3, x, meta, y)


def _moe(x, g, w_router, wg, wu, wd, tm):
    m, d = x.shape
    n_exp = w_router.shape[1]
    w_pad = jnp.zeros((d, LANES), F32).at[:, :n_exp].set(w_router)
    meta, counts = _router(x, g, w_pad, n_exp)
    expert = meta[:, 0:TOP_K].astype(jnp.int32)
    rank = meta[:, 4:4 + TOP_K].astype(jnp.int32)
    cnt = counts[0, :n_exp].astype(jnp.int32)
    padded = (cnt + tm - 1) // tm * tm
    ends = jnp.cumsum(padded)
    dest = (ends - padded)[expert] + rank
    n_tiles = (m * TOP_K) // tm + n_exp
    tile_expert = jnp.minimum(
        jnp.searchsorted(ends // tm, jnp.arange(n_tiles, dtype=jnp.int32), side="right"), n_exp - 1).astype(jnp.int32)
    n_valid = (ends[-1:] // tm).astype(jnp.int32)
    xs = _dispatch(x, dest, n_tiles * tm)
    _, ns = _addnorm(xs, None, g, BF16)
    ys = _ffn(ns, wg, wu, wd, tile_expert, n_valid, tm)
    return _combine(x, meta, ys, dest)


def kernel(x_prompt, x_sample, cache_swa_k, cache_swa_v, state_gla, state_conv, state_lru, ln_mix, w_in, attn_sink, w_gla_lr2, b_gla_gate, g_gla_norm, w_conv, b_conv, w_lru_a, b_lru_a, w_lru_x, b_lru_x, lru_lambda, w_branch, w_gate, b_gate, w_out, ln_ffn, w_ff_gate, w_ff_up, w_ff_down, w_router, w_moe_gate, w_moe_up, w_moe_down, ln_final):
    bp, seq, d = x_prompt.shape
    bs, dseq, _ = x_sample.shape
    depth = ln_mix.shape[0]
    _, _, window, n_kv, hd = cache_swa_k.shape
    n_heads = attn_sink.shape[1]
    groups = n_heads // n_kv
    _, _, gh, dk, dv = state_gla.shape
    lowrank = w_gla_lr2.shape[1]
    conv_w = w_conv.shape[1]
    width = state_lru.shape[2]
    aq, akv, gqk, gv = n_heads * hd, n_kv * hd, gh * dk, gh * dv
    assert dseq == CHUNK and seq % CHUNK == 0 and window == 2 * CHUNK and hd == LANES and lowrank <= LANES
    mp, ms = bp * seq, bs * dseq
    m = mp + ms
    nc = seq // CHUNK
    main = aq + 2 * akv + 2 * gqk + 2 * gv
    c_col = main + lowrank
    assert w_in.shape[2] == c_col + 2 * width

    w_lr = jnp.zeros((depth, d, LANES), BF16).at[:, :, :lowrank].set(w_in[:, :, main:c_col].astype(BF16))
    w_lr2 = jnp.zeros((depth, LANES, gqk), BF16).at[:, :lowrank].set(w_gla_lr2.astype(BF16))
    w_c = w_in[:, :, c_col:]
    cache_k2 = cache_swa_k.reshape(depth, bs * window, akv)
    cache_v2 = cache_swa_v.reshape(depth, bs * window, akv)
    conv_prev_s = jnp.pad(state_conv, ((0, 0), (0, 0), (SUBLANES - (conv_w - 1), 0), (0, 0)))
    conv_prev_p = jnp.zeros((bp, SUBLANES, width), F32)
    wa_bf, wx_bf = w_lru_a.astype(BF16), w_lru_x.astype(BF16)

    n_pos = max(seq, PAST_LEN + dseq)
    cos, sin = _rope_tables(n_pos, hd)

    x = jnp.concatenate([x_prompt.reshape(mp, d), x_sample.reshape(ms, d)], axis=0)
    y = None
    outs = {k: [] for k in ("pk", "pv", "pg", "pc", "pl", "sk", "sv", "sg", "sc", "sl")}
    tm_ffn = FFN_ROW_TILE if m % FFN_ROW_TILE == 0 else _tile(m, FFN_ROW_TILE)
    for l in range(depth):
        x, n = _addnorm(x, y, ln_mix[l], BF16)
        u = _mm(n, w_in, l, 0, main)
        uc = _mm(n, w_c, l, 0, 2 * width)
        lg = _lowrank_gate(n, w_lr[l], w_lr2[l], b_gla_gate[l].reshape(1, gqk))

        swa = functools.partial(_swa, u, attn_sink[l], cos, sin, n_kv=n_kv, groups=groups, hd=hd, m_rows=m, aq=aq)
        oa_p, kr_p = swa(None, None, n_seq=bp, n_chunks=nc, row0=0, pos0=0)
        oa_s, kr_s = swa(cache_k2[l], cache_v2[l], n_seq=bs, n_chunks=1, row0=mp, pos0=PAST_LEN)

        gla = functools.partial(_gla, u, lg, g_gla_norm[l], heads=gh, dk=dk, dv=dv, q_col=aq + 2 * akv,
                                k_col=aq + 2 * akv + gqk, v_col=aq + 2 * akv + 2 * gqk,
                                r_col=aq + 2 * akv + 2 * gqk + gv)
        ob_p, sg_p = gla(None, n_seq=bp, n_chunks=nc, row0=0)
        ob_s, sg_s = gla(state_gla[l], n_seq=bs, n_chunks=1, row0=mp)

        lru = functools.partial(_lru, uc, wc=w_conv[l], bc=b_conv[l].reshape(1, width), wa=wa_bf[l],
                                ba=b_lru_a[l].reshape(1, width), wx=wx_bf[l], bx=b_lru_x[l].reshape(1, width),
                                lam=lru_lambda[l].reshape(1, width), width=width)
        oc_p, hl_p = lru(conv_prev_p, jnp.zeros((bp, 1, width), F32), n_seq=bp, n_chunks=nc, row0=0,
                         stream_start=True)
        oc_s, hl_s = lru(conv_prev_s[l], state_lru[l].reshape(bs, 1, width), n_seq=bs, n_chunks=1, row0=mp,
                         stream_start=False)

        oa = jnp.concatenate([oa_p, oa_s], axis=0)
        ob = jnp.concatenate([ob_p, ob_s], axis=0)
        oc = jnp.concatenate([oc_p, oc_s], axis=0)
        merged = _merge(n, oa, ob, oc, w_gate, b_gate.reshape(depth, 1, -1), w_branch, l)
        y = _mm(merged, w_out, l, 0, d)
        x, n2 = _addnorm(x, y, ln_ffn[l], BF16)
        if l % 2 == 0:
            i = l // 2
            ones = jnp.zeros((m // tm_ffn,), jnp.int32)
            y = _ffn(n2, w_ff_gate[i:i + 1], w_ff_up[i:i + 1], w_ff_down[i:i + 1], ones,
                     jnp.full((1,), m // tm_ffn, jnp.int32), tm_ffn)
        else:
            i = l // 2
            x = _moe(x, ln_ffn[l], w_router[i], w_moe_gate[i], w_moe_up[i], w_moe_down[i], tm_ffn)
            y = None

        vcol = u[:, aq + akv:aq + 2 * akv]
        up3 = u[:mp].reshape(bp, seq, -1)
        outs["pk"].append(kr_p.reshape(bp, seq, n_kv, hd)[:, seq - window:])
        outs["pv"].append(vcol[:mp].reshape(bp, seq, n_kv, hd)[:, seq - window:])
        outs["pg"].append(sg_p)
        outs["pc"].append(uc[:mp, :width].reshape(bp, seq, width)[:, seq - (conv_w - 1):])
        outs["pl"].append(hl_p.reshape(bp, width))
        keep = window - dseq
        outs["sk"].append(jnp.concatenate([cache_swa_k[l][:, window - keep:], kr_s.reshape(bs, dseq, n_kv, hd)], 1))
        outs["sv"].append(jnp.concatenate([cache_swa_v[l][:, window - keep:],
                                           vcol[mp:].reshape(bs, dseq, n_kv, hd)], 1))
        outs["sg"].append(sg_s)
        outs["sc"].append(uc[mp:, :width].reshape(bs, dseq, width)[:, dseq - (conv_w - 1):])
        outs["sl"].append(hl_s.reshape(bs, width))
        del up3

    if y is not None:
        x, yf = _addnorm(x, y, ln_final, F32)
    else:
        _, yf = _addnorm(x, None, ln_final, F32)
    st = {k: jnp.stack(v) for k, v in outs.items()}
    return (yf[:mp].reshape(bp, seq, d), yf[mp:].reshape(bs, dseq, d),
            st["pk"], st["pv"], st["pg"], st["pc"], st["pl"],
            st["sk"], st["sv"], st["sg"], st["sc"], st["sl"])
```

```python
import functools

import jax
import jax.numpy as jnp
from jax import lax
from jax.experimental import pallas as pl
from jax.experimental.pallas import tpu as pltpu

F32 = jnp.float32
BF16 = jnp.bfloat16

CHUNK = 64
PAST_LEN = 2048
TOP_K = 2
EPS = 1e-6
NEG_INF = -1e30
ROPE_THETA = 10000.0
GLA_TAU = 16.0
LRU_C = 8.0

LANES = 128
SUBLANES = 8
V7X_VMEM_BYTES = 64 * 1024 * 1024
MIB = 1024 * 1024
FFN_ROW_TILE = 768
FFN_DOWN_COLS = 1024


def _params(semantics, vmem_mib):
    assert vmem_mib * MIB < V7X_VMEM_BYTES
    return pltpu.CompilerParams(dimension_semantics=semantics, vmem_limit_bytes=vmem_mib * MIB)


def _tile(n, target):
    t = 1
    while t * 2 <= target and n % (t * 2) == 0:
        t *= 2
    return t


def _rope_table_kernel(cos_ref, sin_ref):
    rows, hd = cos_ref.shape
    half = hd // 2
    lane = lax.broadcasted_iota(jnp.int32, (rows, hd), 1)
    j = jnp.where(lane < half, lane, lane - half).astype(F32)
    inv = jnp.power(jnp.float32(ROPE_THETA), -j / half)
    pos = (lax.broadcasted_iota(jnp.int32, (rows, hd), 0) + pl.program_id(0) * rows).astype(F32)
    ang = pos * inv
    cos_ref[...] = jnp.cos(ang)
    s = jnp.sin(ang)
    sin_ref[...] = jnp.where(lane < half, -s, s)


def _rope_tables(n_pos, head_dim):
    rows = _tile(n_pos, 512)
    return pl.pallas_call(
        _rope_table_kernel,
        grid=(n_pos // rows,),
        out_specs=[pl.BlockSpec((rows, head_dim), lambda i: (i, 0))] * 2,
        out_shape=[jax.ShapeDtypeStruct((n_pos, head_dim), F32)] * 2,
        compiler_params=_params(("arbitrary",), 16),
        name="rope_tables",
    )()


def _rms(xf, g):
    return xf * lax.rsqrt(jnp.mean(xf * xf, axis=-1, keepdims=True) + EPS) * g


def _addnorm_kernel(*refs, has_y, emit_x):
    it = iter(refs)
    x_ref = next(it)
    y_ref = next(it) if has_y else None
    g_ref = next(it)
    xo_ref = next(it) if emit_x else None
    n_ref = next(it)
    x = x_ref[...]
    if has_y:
        x = x + y_ref[...]
    if emit_x:
        xo_ref[...] = x
    n_ref[...] = _rms(x, g_ref[...]).astype(n_ref.dtype)


def _addnorm(x, y, g, out_dtype):
    m, d = x.shape
    tm = _tile(m, 256)
    has_y = y is not None
    row = pl.BlockSpec((tm, d), lambda i: (i, 0))
    ins = [x] + ([y] if has_y else []) + [g.reshape(1, d)]
    in_specs = [row] * (2 if has_y else 1) + [pl.BlockSpec((1, d), lambda i: (0, 0))]
    out_shape = ([jax.ShapeDtypeStruct((m, d), F32)] if has_y else []) + [jax.ShapeDtypeStruct((m, d), out_dtype)]
    outs = pl.pallas_call(
        functools.partial(_addnorm_kernel, has_y=has_y, emit_x=has_y),
        grid=(m // tm,),
        in_specs=in_specs,
        out_specs=[row] * len(out_shape),
        out_shape=out_shape,
        compiler_params=_params(("arbitrary",), 48),
        name="add_rmsnorm",
    )(*ins)
    return (outs[0], outs[1]) if has_y else (x, outs[0])


def _mm_kernel(x_ref, w_ref, o_ref, wbf_ref):
    @pl.when(pl.program_id(1) == 0)
    def _():
        wbf_ref[...] = w_ref[...].astype(BF16)

    o_ref[...] = jnp.dot(x_ref[...], wbf_ref[...], preferred_element_type=F32).astype(o_ref.dtype)


def _mm(x, w, layer, col0, n, out_dtype=F32):
    m, k = x.shape
    tm = _tile(m, 1024)
    tn = _tile(n, 512)
    assert col0 % tn == 0 and tn % LANES == 0
    off = col0 // tn
    return pl.pallas_call(
        _mm_kernel,
        grid=(n // tn, m // tm),
        in_specs=[
            pl.BlockSpec((tm, k), lambda j, i: (i, 0)),
            pl.BlockSpec((None, k, tn), lambda j, i: (layer, 0, j + off)),
        ],
        out_specs=pl.BlockSpec((tm, tn), lambda j, i: (i, j)),
        out_shape=jax.ShapeDtypeStruct((m, n), out_dtype),
        scratch_shapes=[pltpu.VMEM((k, tn), BF16)],
        compiler_params=_params(("arbitrary", "arbitrary"), 56),
        name="matmul",
    )(x, w)


def _log_sigmoid(z):
    y = -z
    return -(jnp.maximum(y, 0.0) + jnp.log1p(jnp.exp(-jnp.abs(y))))


def _lowrank_kernel(n_ref, w1_ref, w2_ref, b_ref, o_ref):
    glr = jnp.dot(n_ref[...], w1_ref[...], preferred_element_type=F32)
    z = jnp.dot(glr.astype(BF16), w2_ref[...], preferred_element_type=F32) + b_ref[...]
    o_ref[...] = _log_sigmoid(z) / GLA_TAU


def _lowrank_gate(n, w1, w2, b):
    m, d = n.shape
    gqk = w2.shape[1]
    tm = _tile(m, 512)
    return pl.pallas_call(
        _lowrank_kernel,
        grid=(m // tm,),
        in_specs=[
            pl.BlockSpec((tm, d), lambda i: (i, 0)),
            pl.BlockSpec(w1.shape, lambda i: (0, 0)),
            pl.BlockSpec(w2.shape, lambda i: (0, 0)),
            pl.BlockSpec((1, gqk), lambda i: (0, 0)),
        ],
        out_specs=pl.BlockSpec((tm, gqk), lambda i: (i, 0)),
        out_shape=jax.ShapeDtypeStruct((m, gqk), F32),
        compiler_params=_params(("arbitrary",), 32),
        name="gla_forget_gate",
    )(n, w1, w2, b)


def _rope(x, cos, sin):
    return x * cos + pltpu.roll(x, x.shape[-1] // 2, 1) * sin


def _swa_kernel(sink_ref, q_ref, k0_ref, k1_ref, k2_ref, v0_ref, v1_ref, v2_ref,
                c0_ref, c1_ref, c2_ref, s0_ref, s1_ref, s2_ref, o_ref, krot_ref, *, groups, cached):
    c = pl.program_id(1)
    hd = c2_ref.shape[-1]
    n_kv = k2_ref.shape[-1] // hd
    cosq, sinq = c2_ref[...], s2_ref[...]
    for kvh in range(n_kv):
        ks = slice(kvh * hd, (kvh + 1) * hd)
        k2 = _rope(k2_ref[:, ks], cosq, sinq)
        krot_ref[:, ks] = k2
        if cached:
            k0, k1 = k0_ref[:, ks], k1_ref[:, ks]
        else:
            k0 = _rope(k0_ref[:, ks], c0_ref[...], s0_ref[...])
            k1 = _rope(k1_ref[:, ks], c1_ref[...], s1_ref[...])
        kall = jnp.concatenate([k0, k1, k2], axis=0).astype(BF16)
        vall = jnp.concatenate([v0_ref[:, ks], v1_ref[:, ks], v2_ref[:, ks]], axis=0).astype(BF16)
        q0 = kvh * groups * hd
        qall = jnp.concatenate(
            [_rope(q_ref[:, q0 + g * hd:q0 + (g + 1) * hd], cosq, sinq) for g in range(groups)], axis=0).astype(BF16)
        s = lax.dot_general(qall, kall, (((1,), (1,)), ((), ())), preferred_element_type=F32) * hd ** -0.5
        if not cached:
            col = lax.broadcasted_iota(jnp.int32, s.shape, 1)
            s = jnp.where(col >= jnp.maximum(2 - c, 0) * CHUNK, s, NEG_INF)
        sink = jnp.concatenate(
            [jnp.full((CHUNK, 1), sink_ref[kvh * groups + g], F32) for g in range(groups)], axis=0)
        mx = jnp.maximum(jnp.max(s, axis=-1, keepdims=True), sink)
        p = jnp.exp(s - mx)
        den = jnp.sum(p, axis=-1, keepdims=True) + jnp.exp(sink - mx)
        p = p / den
        o = jnp.dot(p.astype(BF16), vall, preferred_element_type=F32)
        for g in range(groups):
            o_ref[:, q0 + g * hd:q0 + (g + 1) * hd] = o[g * CHUNK:(g + 1) * CHUNK].astype(o_ref.dtype)


def _swa(u, sink, cos, sin, cache_k, cache_v, *, n_seq, n_chunks, row0, n_kv, groups, hd, pos0, aq):
    cached = cache_k is not None
    akv = n_kv * hd
    assert aq % akv == 0
    rb0 = row0 // CHUNK
    pb0 = pos0 // CHUNK
    kcol, vcol = aq // akv, aq // akv + 1

    def rowblk(b, c):
        return rb0 + b * n_chunks + c

    def window(colblk):
        return [pl.BlockSpec((CHUNK, akv), lambda b, c, j=j: (rowblk(b, jnp.maximum(c - 2 + j, 0)), colblk))
                for j in range(2)]

    if cached:
        kprev = vprev = [pl.BlockSpec((CHUNK, akv), lambda b, c, j=j: (2 * b + j, 0)) for j in range(2)]
        kin, vin = [cache_k, cache_k], [cache_v, cache_v]
    else:
        kprev, vprev = window(kcol), window(vcol)
        kin = vin = [u, u]
    q_spec = pl.BlockSpec((CHUNK, aq), lambda b, c: (rowblk(b, c), 0))
    kcur = pl.BlockSpec((CHUNK, akv), lambda b, c: (rowblk(b, c), kcol))
    vcur = pl.BlockSpec((CHUNK, akv), lambda b, c: (rowblk(b, c), vcol))
    tab = [pl.BlockSpec((CHUNK, hd), lambda b, c, j=j: (pb0 + jnp.maximum(c - 2 + j, 0), 0)) for j in range(3)]
    out_rows = n_seq * n_chunks * CHUNK
    return pl.pallas_call(
        functools.partial(_swa_kernel, groups=groups, cached=cached),
        grid=(n_seq, n_chunks),
        in_specs=[pl.BlockSpec(memory_space=pltpu.SMEM), q_spec] + kprev + [kcur] + vprev + [vcur] + tab + tab,
        out_specs=[pl.BlockSpec((CHUNK, aq), lambda b, c: (b * n_chunks + c, 0)),
                   pl.BlockSpec((CHUNK, akv), lambda b, c: (b * n_chunks + c, 0))],
        out_shape=[jax.ShapeDtypeStruct((out_rows, aq), BF16), jax.ShapeDtypeStruct((out_rows, akv), F32)],
        compiler_params=_params(("arbitrary",) * 2, 32),
        name="swa_cached" if cached else "swa_prompt",
    )(sink, u, *kin, u, *vin, u, cos, cos, cos, sin, sin, sin)


def _cumsum_rows(x):
    row = lax.broadcasted_iota(jnp.int32, x.shape, 0)
    d = 1
    while d < x.shape[0]:
        x = x + jnp.where(row >= d, pltpu.roll(x, d, 0), 0.0)
        d *= 2
    return x


def _gla_kernel(*refs, heads, has_init):
    it = iter(refs)
    q_ref, k_ref, lg_ref, g_ref = (next(it) for _ in range(4))
    v_refs = [next(it) for _ in range(heads)]
    r_refs = [next(it) for _ in range(heads)]
    s0_ref = next(it) if has_init else None
    o_ref, sout_ref, st_ref = next(it), next(it), next(it)
    c = pl.program_id(1)
    dk = q_ref.shape[-1] // heads
    dv = v_refs[0].shape[-1]

    @pl.when(c == 0)
    def _():
        for h in range(heads):
            st_ref[h] = s0_ref[h].T if has_init else jnp.zeros((dv, dk), F32)

    tri = lax.broadcasted_iota(jnp.int32, (CHUNK, CHUNK), 0) >= lax.broadcasted_iota(jnp.int32, (CHUNK, CHUNK), 1)
    for h in range(heads):
        ks = slice(h * dk, (h + 1) * dk)
        b = _cumsum_rows(lg_ref[:, ks])
        b_last = b[CHUNK - 1:CHUNK, :]
        k = k_ref[:, ks]
        qe = (q_ref[:, ks] * dk ** -0.5 * jnp.exp(b)).astype(BF16)
        ke = (k * jnp.exp(-b)).astype(BF16)
        kd = (k * jnp.exp(b_last - b)).astype(BF16)
        v = v_refs[h][...].astype(BF16)
        att = lax.dot_general(qe, ke, (((1,), (1,)), ((), ())), preferred_element_type=F32)
        att = jnp.where(tri, att, 0.0)
        st = st_ref[h]
        o = lax.dot_general(qe, st.astype(BF16), (((1,), (1,)), ((), ())), preferred_element_type=F32)
        o = o + jnp.dot(att.astype(BF16), v, preferred_element_type=F32)
        st_new = jnp.exp(b_last) * st + lax.dot_general(v, kd, (((0,), (0,)), ((), ())), preferred_element_type=F32)
        st_ref[h] = st_new
        r = r_refs[h][...]
        o_ref[:, h * dv:(h + 1) * dv] = (_rms(o, g_ref[...]) * (r * jax.nn.sigmoid(r))).astype(o_ref.dtype)

    @pl.when(c == pl.num_programs(1) - 1)
    def _():
        for h in range(heads):
            sout_ref[h] = st_ref[h].T


def _gla(u, lg, g_norm, s0, *, n_seq, n_chunks, row0, heads, dk, dv, q_col, k_col, v_col, r_col):
    has_init = s0 is not None
    rb0 = row0 // CHUNK
    gqk = heads * dk
    assert q_col % gqk == 0 and k_col % gqk == 0 and v_col % dv == 0 and r_col % dv == 0

    def col(width, blk):
        return pl.BlockSpec((CHUNK, width), lambda b, c: (rb0 + b * n_chunks + c, blk))

    in_specs = ([col(gqk, q_col // gqk), col(gqk, k_col // gqk), col(gqk, 0), pl.BlockSpec((1, dv), lambda b, c: (0, 0))]
                + [col(dv, v_col // dv + h) for h in range(heads)] + [col(dv, r_col // dv + h) for h in range(heads)])
    ins = [u, u, lg, g_norm.reshape(1, dv)] + [u] * (2 * heads)
    if has_init:
        in_specs.append(pl.BlockSpec((None, heads, dk, dv), lambda b, c: (b, 0, 0, 0)))
        ins.append(s0)
    out_rows = n_seq * n_chunks * CHUNK
    return pl.pallas_call(
        functools.partial(_gla_kernel, heads=heads, has_init=has_init),
        grid=(n_seq, n_chunks),
        in_specs=in_specs,
        out_specs=[pl.BlockSpec((CHUNK, heads * dv), lambda b, c: (b * n_chunks + c, 0)),
                   pl.BlockSpec((None, heads, dk, dv), lambda b, c: (b, 0, 0, 0))],
        out_shape=[jax.ShapeDtypeStruct((out_rows, heads * dv), BF16),
                   jax.ShapeDtypeStruct((n_seq, heads, dk, dv), F32)],
        scratch_shapes=[pltpu.VMEM((heads, dv, dk), F32)],
        compiler_params=_params(("arbitrary",) * 2, 40),
        name="gla_init" if has_init else "gla_prompt",
    )(*ins)


def _softplus(y):
    return jnp.maximum(y, 0.0) + jnp.log1p(jnp.exp(-jnp.abs(y)))


def _lru_kernel(x_ref, xprev_ref, cprev_ref, y_ref, wc_ref, bc_ref, wa_ref, ba_ref, wx_ref, bx_ref, lam_ref, h0_ref,
                o_ref, hl_ref, h_ref, a_ref, u_ref, *, stream_start, conv_w):
    c = pl.program_id(1)
    t, w = x_ref.shape
    nblk, bw = wa_ref.shape[0], wa_ref.shape[1]

    @pl.when(c == 0)
    def _():
        h_ref[...] = h0_ref[...]

    prev = jnp.where(c == 0, cprev_ref[...], xprev_ref[...])
    ext = jnp.concatenate([prev, x_ref[...]], axis=0)
    xc = bc_ref[...]
    for j in range(conv_w):
        shift = conv_w - 1 - j
        xs = ext if shift == 0 else pltpu.roll(ext, shift, 0)
        xc = xc + xs[SUBLANES:] * wc_ref[j:j + 1, :]
    xb = xc.astype(BF16)
    ra = jnp.concatenate([jnp.dot(xb[:, n * bw:(n + 1) * bw], wa_ref[n], preferred_element_type=F32)
                          for n in range(nblk)], axis=1)
    ia = jnp.concatenate([jnp.dot(xb[:, n * bw:(n + 1) * bw], wx_ref[n], preferred_element_type=F32)
                          for n in range(nblk)], axis=1)
    r = jax.nn.sigmoid(ra + ba_ref[...])
    gate = jax.nn.sigmoid(ia + bx_ref[...])
    log_a = -LRU_C * r * _softplus(-lam_ref[...])
    a = jnp.exp(log_a)
    mult = jnp.sqrt(jnp.tanh(-log_a) * (a * a + 1.0))
    if stream_start:
        first = (lax.broadcasted_iota(jnp.int32, (t, w), 0) == 0) & (c == 0)
        mult = jnp.where(first, 1.0, mult)
        a = jnp.where(first, 0.0, a)
    a_ref[...] = a
    u_ref[...] = mult * gate * xc

    def step(i, h):
        h = a_ref[pl.ds(i, 1), :] * h + u_ref[pl.ds(i, 1), :]
        u_ref[pl.ds(i, 1), :] = h
        return h

    h = lax.fori_loop(0, t, step, h_ref[...], unroll=8)
    h_ref[...] = h
    hl_ref[...] = h
    o_ref[...] = (jax.nn.gelu(y_ref[...]) * u_ref[...]).astype(o_ref.dtype)


def _lru(uc, cprev8, h0, wc, bc, wa, ba, wx, bx, lam, *, n_seq, n_chunks, row0, width, stream_start):
    rb0 = row0 // CHUNK
    per = CHUNK // SUBLANES
    conv_w = wc.shape[0]
    vec = pl.BlockSpec((1, width), lambda b, c: (0, 0))
    blk = pl.BlockSpec(wa.shape, lambda b, c: (0, 0, 0))
    out_rows = n_seq * n_chunks * CHUNK
    return pl.pallas_call(
        functools.partial(_lru_kernel, stream_start=stream_start, conv_w=conv_w),
        grid=(n_seq, n_chunks),
        in_specs=[
            pl.BlockSpec((CHUNK, width), lambda b, c: (rb0 + b * n_chunks + c, 0)),
            pl.BlockSpec((SUBLANES, width), lambda b, c: (jnp.maximum((rb0 + b * n_chunks + c) * per - 1, 0), 0)),
            pl.BlockSpec((None, SUBLANES, width), lambda b, c: (b, 0, 0)),
            pl.BlockSpec((CHUNK, width), lambda b, c: (rb0 + b * n_chunks + c, 1)),
            pl.BlockSpec(wc.shape, lambda b, c: (0, 0)), vec, blk, vec, blk, vec, vec,
            pl.BlockSpec((None, 1, width), lambda b, c: (b, 0, 0)),
        ],
        out_specs=[pl.BlockSpec((CHUNK, width), lambda b, c: (b * n_chunks + c, 0)),
                   pl.BlockSpec((None, 1, width), lambda b, c: (b, 0, 0))],
        out_shape=[jax.ShapeDtypeStruct((out_rows, width), BF16), jax.ShapeDtypeStruct((n_seq, 1, width), F32)],
        scratch_shapes=[pltpu.VMEM((1, width), F32), pltpu.VMEM((CHUNK, width), F32), pltpu.VMEM((CHUNK, width), F32)],
        compiler_params=_params(("arbitrary",) * 2, 32),
        name="conv_rglru",
    )(uc, uc, cprev8, uc, wc, bc, wa, ba, wx, bx, lam, h0)


def _merge_kernel(n_ref, ap_ref, bp_ref, cp_ref, as_ref, bs_ref, cs_ref, wg0_ref, wg1_ref, wg2_ref, wb_ref,
                  bg0_ref, bg1_ref, bg2_ref, o_ref, wgbf_ref, wbbf_ref, *, prompt_tiles):
    i = pl.program_id(1)

    @pl.when(i == 0)
    def _():
        for b, wg_ref in enumerate((wg0_ref, wg1_ref, wg2_ref)):
            wgbf_ref[b] = wg_ref[...].astype(BF16)
            wbbf_ref[b] = wb_ref[b].astype(BF16)

    n = n_ref[...]
    acc = None
    branches = ((ap_ref, as_ref, bg0_ref), (bp_ref, bs_ref, bg1_ref), (cp_ref, cs_ref, bg2_ref))
    for b, (p_ref, s_ref, bg_ref) in enumerate(branches):
        br = jnp.where(i < prompt_tiles, p_ref[...], s_ref[...])
        gate = jax.nn.sigmoid(jnp.dot(n, wgbf_ref[b], preferred_element_type=F32) + bg_ref[...])
        term = gate * jnp.dot(br, wbbf_ref[b], preferred_element_type=F32)
        acc = term if acc is None else acc + term
    o_ref[...] = acc.astype(o_ref.dtype)


def _merge(n, prompt, sample, w_gate, b_gate, w_branch, layer):
    m, d = n.shape
    mp, bw = prompt[0].shape
    ms = sample[0].shape[0]
    nb = w_branch.shape[1]
    assert nb == 3 and mp + ms == m
    tm = min(_tile(mp, 512), _tile(ms, 512))
    tn = _tile(d, 256)
    nj = d // tn
    pt = mp // tm
    single = pl.Buffered(1)
    wg_specs = [pl.BlockSpec((None, d, tn), lambda j, i, b=b: (layer, 0, b * nj + j), pipeline_mode=single)
                for b in range(nb)]
    bg_specs = [pl.BlockSpec((None, 1, tn), lambda j, i, b=b: (layer, 0, b * nj + j)) for b in range(nb)]
    p_spec = pl.BlockSpec((tm, bw), lambda j, i: (jnp.minimum(i, pt - 1), 0))
    s_spec = pl.BlockSpec((tm, bw), lambda j, i: (jnp.maximum(i - pt, 0), 0), pipeline_mode=single)
    return pl.pallas_call(
        functools.partial(_merge_kernel, prompt_tiles=pt),
        grid=(nj, m // tm),
        in_specs=[pl.BlockSpec((tm, d), lambda j, i: (i, 0))] + [p_spec] * nb + [s_spec] * nb + wg_specs
        + [pl.BlockSpec((None, nb, bw, tn), lambda j, i: (layer, 0, 0, j), pipeline_mode=single)] + bg_specs,
        out_specs=pl.BlockSpec((tm, tn), lambda j, i: (i, j)),
        out_shape=jax.ShapeDtypeStruct((m, d), BF16),
        scratch_shapes=[pltpu.VMEM((nb, d, tn), BF16), pltpu.VMEM((nb, bw, tn), BF16)],
        compiler_params=_params(("arbitrary", "arbitrary"), 60),
        name="gated_merge",
    )(n, *prompt, *sample, w_gate, w_gate, w_gate, w_branch, b_gate, b_gate, b_gate)


def _ffn_kernel(te_ref, nv_ref, x_ref, wg_ref, wu_ref, wd_ref, o_ref):
    i, j = pl.program_id(0), pl.program_id(1)

    @pl.when(i < nv_ref[0])
    def _():
        x = x_ref[...]
        g = jnp.dot(x, wg_ref[...].astype(BF16), preferred_element_type=F32)
        u = jnp.dot(x, wu_ref[...].astype(BF16), preferred_element_type=F32)
        h = (g * jax.nn.sigmoid(g) * u).astype(BF16)
        d = o_ref.shape[1]
        dc = min(d, FFN_DOWN_COLS)

        def down(n0):
            return jnp.dot(h, wd_ref[:, n0:n0 + dc].astype(BF16), preferred_element_type=F32)

        @pl.when(j == 0)
        def _():
            for n0 in range(0, d, dc):
                o_ref[:, n0:n0 + dc] = down(n0)

        @pl.when(j > 0)
        def _():
            for n0 in range(0, d, dc):
                o_ref[:, n0:n0 + dc] += down(n0)

    @pl.when((i >= nv_ref[0]) & (j == 0))
    def _():
        o_ref[...] = jnp.zeros_like(o_ref)


def _ffn(x, wg, wu, wd, tile_expert, n_valid, tm):
    r, d = x.shape
    f = wg.shape[2]
    tf = _tile(f, 256)
    nj = f // tf
    single = pl.Buffered(1)

    def row(i, j, te, nv):
        return (jnp.minimum(i, nv[0] - 1), 0)

    def fidx(i, j, nv):
        return jnp.where(i < nv[0], j, nj - 1)

    def up(i, j, te, nv):
        return (te[jnp.minimum(i, nv[0] - 1)], 0, fidx(i, j, nv))

    def down(i, j, te, nv):
        return (te[jnp.minimum(i, nv[0] - 1)], fidx(i, j, nv), 0)

    return pl.pallas_call(
        _ffn_kernel,
        grid_spec=pltpu.PrefetchScalarGridSpec(
            num_scalar_prefetch=2,
            grid=(r // tm, nj),
            in_specs=[pl.BlockSpec((tm, d), row, pipeline_mode=single),
                      pl.BlockSpec((None, d, tf), up), pl.BlockSpec((None, d, tf), up),
                      pl.BlockSpec((None, tf, d), down)],
            out_specs=pl.BlockSpec((tm, d), lambda i, j, te, nv: (i, 0), pipeline_mode=single),
        ),
        out_shape=jax.ShapeDtypeStruct((r, d), F32),
        compiler_params=_params(("arbitrary", "arbitrary"), 60),
        name="swiglu_ffn",
    )(tile_expert, n_valid, x, wg, wu, wd)


def _router_kernel(x_ref, g_ref, w_ref, meta_ref, cnt_ref, carry_ref, *, n_experts):
    @pl.when(pl.program_id(0) == 0)
    def _():
        carry_ref[...] = jnp.zeros_like(carry_ref)

    tm = x_ref.shape[0]
    n = _rms(x_ref[...], g_ref[...])
    logits = jnp.dot(n, w_ref[...], preferred_element_type=F32, precision=lax.Precision.HIGHEST)
    lane = lax.broadcasted_iota(jnp.int32, logits.shape, 1).astype(F32)
    logits = jnp.where(lane < n_experts, logits, -jnp.inf)
    m1 = jnp.max(logits, axis=-1, keepdims=True)
    i1 = jnp.min(jnp.where(logits == m1, lane, float(LANES)), axis=-1, keepdims=True)
    rest = jnp.where(lane == i1, -jnp.inf, logits)
    m2 = jnp.max(rest, axis=-1, keepdims=True)
    i2 = jnp.min(jnp.where(rest == m2, lane, float(LANES)), axis=-1, keepdims=True)
    e2 = jnp.exp(m2 - m1)
    den = 1.0 + e2
    oh1 = lane == i1
    oh2 = lane == i2
    both = jnp.where(oh1 | oh2, 1.0, 0.0)
    rr = lax.broadcasted_iota(jnp.int32, (tm, tm), 0)
    cc = lax.broadcasted_iota(jnp.int32, (tm, tm), 1)
    below = jnp.where(rr > cc, 1.0, 0.0).astype(BF16)
    before = jnp.dot(below, both.astype(BF16), preferred_element_type=F32) + carry_ref[...]
    rank1 = jnp.sum(jnp.where(oh1, before, 0.0), axis=-1, keepdims=True)
    rank2 = jnp.sum(jnp.where(oh2, before, 0.0), axis=-1, keepdims=True)
    cols = (i1, i2, 1.0 / den, e2 / den, rank1, rank2)
    meta = jnp.zeros(logits.shape, F32)
    for idx, val in enumerate(cols):
        meta = jnp.where(lane == idx, val, meta)
    meta_ref[...] = meta
    total = carry_ref[...] + jnp.sum(both, axis=0, keepdims=True)
    carry_ref[...] = total
    cnt_ref[...] = total


def _router(x, g, w_pad, n_experts):
    m, d = x.shape
    tm = _tile(m, 256)
    return pl.pallas_call(
        functools.partial(_router_kernel, n_experts=n_experts),
        grid=(m // tm,),
        in_specs=[pl.BlockSpec((tm, d), lambda i: (i, 0)), pl.BlockSpec((1, d), lambda i: (0, 0)),
                  pl.BlockSpec(w_pad.shape, lambda i: (0, 0))],
        out_specs=[pl.BlockSpec((tm, LANES), lambda i: (i, 0)), pl.BlockSpec((1, LANES), lambda i: (0, 0))],
        out_shape=[jax.ShapeDtypeStruct((m, LANES), F32), jax.ShapeDtypeStruct((1, LANES), F32)],
        scratch_shapes=[pltpu.VMEM((1, LANES), F32)],
        compiler_params=_params(("arbitrary",), 40),
        name="moe_router",
    )(x, g.reshape(1, d), w_pad)


def _dispatch_kernel(src_ref, x_hbm, g_ref, o_ref, buf_ref, sem):
    rows = buf_ref.shape[0]

    def copy(r):
        return pltpu.make_async_copy(x_hbm.at[pl.ds(src_ref[0, r], 1)], buf_ref.at[pl.ds(r, 1)], sem)

    def start(r, _):
        copy(r).start()
        return 0

    def wait(r, _):
        copy(r).wait()
        return 0

    lax.fori_loop(0, rows, start, 0)
    lax.fori_loop(0, rows, wait, 0)
    o_ref[...] = _rms(buf_ref[...], g_ref[...]).astype(o_ref.dtype)


def _dispatch(x, g, src):
    m, d = x.shape
    n_rows = src.shape[0]
    rows = _tile(n_rows, 256)
    return pl.pallas_call(
        _dispatch_kernel,
        grid=(n_rows // rows,),
        in_specs=[pl.BlockSpec((None, 1, rows), lambda i: (i, 0, 0), memory_space=pltpu.SMEM),
                  pl.BlockSpec(memory_space=pl.ANY), pl.BlockSpec((1, d), lambda i: (0, 0))],
        out_specs=pl.BlockSpec((rows, d), lambda i: (i, 0)),
        out_shape=jax.ShapeDtypeStruct((n_rows, d), BF16),
        scratch_shapes=[pltpu.VMEM((rows, d), F32), pltpu.SemaphoreType.DMA(())],
        compiler_params=_params(("arbitrary",), 32),
        name="moe_dispatch",
    )(src.reshape(n_rows // rows, 1, rows), x, g.reshape(1, d))


def _combine_kernel(dest_ref, x_ref, meta_ref, y_hbm, o_ref, ya_ref, yb_ref, sem, *, rows):
    bufs = (ya_ref, yb_ref)

    def copy(r, k):
        return pltpu.make_async_copy(y_hbm.at[pl.ds(dest_ref[0, TOP_K * r + k], 1)], bufs[k].at[pl.ds(r, 1)], sem)

    def start(r, _):
        for k in range(TOP_K):
            copy(r, k).start()
        return 0

    def wait(r, _):
        for k in range(TOP_K):
            copy(r, k).wait()
        return 0

    lax.fori_loop(0, rows, start, 0)
    lax.fori_loop(0, rows, wait, 0)
    meta = meta_ref[...]
    o_ref[...] = x_ref[...] + (meta[:, 2:3] * ya_ref[...] + meta[:, 3:4] * yb_ref[...])


def _combine(x, meta, y, dest):
    m, d = x.shape
    rows = _tile(m, 128)
    dest3 = dest.reshape(m // rows, 1, rows * TOP_K)
    return pl.pallas_call(
        functools.partial(_combine_kernel, rows=rows),
        grid=(m // rows,),
        in_specs=[pl.BlockSpec((None, 1, rows * TOP_K), lambda i: (i, 0, 0), memory_space=pltpu.SMEM),
                  pl.BlockSpec((rows, d), lambda i: (i, 0)), pl.BlockSpec((rows, LANES), lambda i: (i, 0)),
                  pl.BlockSpec(memory_space=pl.ANY)],
        out_specs=pl.BlockSpec((rows, d), lambda i: (i, 0)),
        out_shape=jax.ShapeDtypeStruct((m, d), F32),
        scratch_shapes=[pltpu.VMEM((rows, d), F32), pltpu.VMEM((rows, d), F32), pltpu.SemaphoreType.DMA(())],
        compiler_params=_params(("arbitrary",), 32),
        name="moe_combine",
    )(dest3, x, meta, y)


def _moe(x, g, w_router, wg, wu, wd, tm):
    m, d = x.shape
    n_exp = w_router.shape[1]
    w_pad = jnp.zeros((d, LANES), F32).at[:, :n_exp].set(w_router)
    meta, counts = _router(x, g, w_pad, n_exp)
    expert = meta[:, 0:TOP_K].astype(jnp.int32)
    rank = meta[:, 4:4 + TOP_K].astype(jnp.int32)
    cnt = counts[0, :n_exp].astype(jnp.int32)
    padded = (cnt + tm - 1) // tm * tm
    ends = jnp.cumsum(padded)
    dest = (ends - padded)[expert] + rank
    n_tiles = (m * TOP_K) // tm + n_exp
    tile_ids = jnp.arange(n_tiles, dtype=jnp.int32)
    tile_expert = jnp.minimum(jnp.sum((ends // tm)[None, :] <= tile_ids[:, None], axis=1), n_exp - 1).astype(jnp.int32)
    n_valid = (ends[-1:] // tm).astype(jnp.int32)
    token = jnp.broadcast_to(jnp.arange(m, dtype=jnp.int32)[:, None], (m, TOP_K))
    src = jnp.zeros((n_tiles * tm,), jnp.int32).at[dest.reshape(-1)].set(token.reshape(-1))
    ns = _dispatch(x, g, src)
    ys = _ffn(ns, wg, wu, wd, tile_expert, n_valid, tm)
    return _combine(x, meta, ys, dest)


def kernel(x_prompt, x_sample, cache_swa_k, cache_swa_v, state_gla, state_conv, state_lru, ln_mix, w_in, attn_sink, w_gla_lr2, b_gla_gate, g_gla_norm, w_conv, b_conv, w_lru_a, b_lru_a, w_lru_x, b_lru_x, lru_lambda, w_branch, w_gate, b_gate, w_out, ln_ffn, w_ff_gate, w_ff_up, w_ff_down, w_router, w_moe_gate, w_moe_up, w_moe_down, ln_final):
    bp, seq, d = x_prompt.shape
    bs, dseq, _ = x_sample.shape
    depth = ln_mix.shape[0]
    _, _, window, n_kv, hd = cache_swa_k.shape
    n_heads = attn_sink.shape[1]
    groups = n_heads // n_kv
    _, _, gh, dk, dv = state_gla.shape
    lowrank = w_gla_lr2.shape[1]
    conv_w = w_conv.shape[1]
    width = state_lru.shape[2]
    aq, akv, gqk, gv = n_heads * hd, n_kv * hd, gh * dk, gh * dv
    assert dseq == CHUNK and seq % CHUNK == 0 and window == 2 * CHUNK and hd == LANES and lowrank <= LANES
    mp, ms = bp * seq, bs * dseq
    m = mp + ms
    nc = seq // CHUNK
    main = aq + 2 * akv + 2 * gqk + 2 * gv
    c_col = main + lowrank
    assert w_in.shape[2] == c_col + 2 * width

    w_lr = jnp.zeros((depth, d, LANES), BF16).at[:, :, :lowrank].set(w_in[:, :, main:c_col].astype(BF16))
    w_lr2 = jnp.zeros((depth, LANES, gqk), BF16).at[:, :lowrank].set(w_gla_lr2.astype(BF16))
    w_c = w_in[:, :, c_col:]
    cache_k2 = cache_swa_k.reshape(depth, bs * window, akv)
    cache_v2 = cache_swa_v.reshape(depth, bs * window, akv)
    conv_prev_s = jnp.pad(state_conv, ((0, 0), (0, 0), (SUBLANES - (conv_w - 1), 0), (0, 0)))
    conv_prev_p = jnp.zeros((bp, SUBLANES, width), F32)
    wa_bf, wx_bf = w_lru_a.astype(BF16), w_lru_x.astype(BF16)

    n_pos = max(seq, PAST_LEN + dseq)
    cos, sin = _rope_tables(n_pos, hd)

    x = jnp.concatenate([x_prompt.reshape(mp, d), x_sample.reshape(ms, d)], axis=0)
    y = None
    outs = {k: [] for k in ("pk", "pv", "pg", "pc", "pl", "sk", "sv", "sg", "sc", "sl")}
    tm_ffn = FFN_ROW_TILE if m % FFN_ROW_TILE == 0 else _tile(m, FFN_ROW_TILE)
    for l in range(depth):
        x, n = _addnorm(x, y, ln_mix[l], BF16)
        u = _mm(n, w_in, l, 0, main)
        uc = _mm(n, w_c, l, 0, 2 * width)
        lg = _lowrank_gate(n, w_lr[l], w_lr2[l], b_gla_gate[l].reshape(1, gqk))

        swa = functools.partial(_swa, u, attn_sink[l], cos, sin, n_kv=n_kv, groups=groups, hd=hd, aq=aq)
        oa_p, kr_p = swa(None, None, n_seq=bp, n_chunks=nc, row0=0, pos0=0)
        oa_s, kr_s = swa(cache_k2[l], cache_v2[l], n_seq=bs, n_chunks=1, row0=mp, pos0=PAST_LEN)

        gla = functools.partial(_gla, u, lg, g_gla_norm[l], heads=gh, dk=dk, dv=dv, q_col=aq + 2 * akv,
                                k_col=aq + 2 * akv + gqk, v_col=aq + 2 * akv + 2 * gqk,
                                r_col=aq + 2 * akv + 2 * gqk + gv)
        ob_p, sg_p = gla(None, n_seq=bp, n_chunks=nc, row0=0)
        ob_s, sg_s = gla(state_gla[l], n_seq=bs, n_chunks=1, row0=mp)

        lru = functools.partial(_lru, uc, wc=w_conv[l], bc=b_conv[l].reshape(1, width), wa=wa_bf[l],
                                ba=b_lru_a[l].reshape(1, width), wx=wx_bf[l], bx=b_lru_x[l].reshape(1, width),
                                lam=lru_lambda[l].reshape(1, width), width=width)
        oc_p, hl_p = lru(conv_prev_p, jnp.zeros((bp, 1, width), F32), n_seq=bp, n_chunks=nc, row0=0,
                         stream_start=True)
        oc_s, hl_s = lru(conv_prev_s[l], state_lru[l].reshape(bs, 1, width), n_seq=bs, n_chunks=1, row0=mp,
                         stream_start=False)

        merged = _merge(n, (oa_p, ob_p, oc_p), (oa_s, ob_s, oc_s), w_gate, b_gate.reshape(depth, 1, -1), w_branch, l)
        y = _mm(merged, w_out, l, 0, d)
        x, n2 = _addnorm(x, y, ln_ffn[l], BF16)
        if l % 2 == 0:
            i = l // 2
            ones = jnp.zeros((m // tm_ffn,), jnp.int32)
            y = _ffn(n2, w_ff_gate[i:i + 1], w_ff_up[i:i + 1], w_ff_down[i:i + 1], ones,
                     jnp.full((1,), m // tm_ffn, jnp.int32), tm_ffn)
        else:
            i = l // 2
            x = _moe(x, ln_ffn[l], w_router[i], w_moe_gate[i], w_moe_up[i], w_moe_down[i], tm_ffn)
            y = None

        def tail_rows(arr, b, t, keep_rows, c0, c1):
            return jnp.stack([arr[i * t + t - keep_rows:(i + 1) * t, c0:c1] for i in range(b)])

        v0, v1 = aq + akv, aq + 2 * akv
        keep = window - dseq
        outs["pk"].append(tail_rows(kr_p, bp, seq, window, 0, akv).reshape(bp, window, n_kv, hd))
        outs["pv"].append(tail_rows(u, bp, seq, window, v0, v1).reshape(bp, window, n_kv, hd))
        outs["pg"].append(sg_p)
        outs["pc"].append(tail_rows(uc, bp, seq, conv_w - 1, 0, width))
        outs["pl"].append(hl_p.reshape(bp, width))
        outs["sk"].append(jnp.concatenate([cache_swa_k[l][:, window - keep:], kr_s.reshape(bs, dseq, n_kv, hd)], 1))
        outs["sv"].append(jnp.concatenate([cache_swa_v[l][:, window - keep:],
                                           u[mp:, v0:v1].reshape(bs, dseq, n_kv, hd)], 1))
        outs["sg"].append(sg_s)
        outs["sc"].append(uc[mp:, :width].reshape(bs, dseq, width)[:, dseq - (conv_w - 1):])
        outs["sl"].append(hl_s.reshape(bs, width))

    if y is not None:
        x, yf = _addnorm(x, y, ln_final, F32)
    else:
        _, yf = _addnorm(x, None, ln_final, F32)
    st = {k: jnp.stack(v) for k, v in outs.items()}
    return (yf[:mp].reshape(bp, seq, d), yf[mp:].reshape(bs, dseq, d),
            st["pk"], st["pv"], st["pg"], st["pc"], st["pl"],
            st["sk"], st["sv"], st["sg"], st["sc"], st["sl"])
```

```python
import functools

import jax
import jax.numpy as jnp
from jax import lax
from jax.experimental import pallas as pl
from jax.experimental.pallas import tpu as pltpu

F32 = jnp.float32
BF16 = jnp.bfloat16

CHUNK = 64
PAST_LEN = 2048
TOP_K = 2
EPS = 1e-6
NEG_INF = -1e30
ROPE_THETA = 10000.0
GLA_TAU = 16.0
LRU_C = 8.0

LANES = 128
SUBLANES = 8
V7X_VMEM_BYTES = 64 * 1024 * 1024
MIB = 1024 * 1024
FFN_ROW_TILE = 768
FFN_DOWN_COLS = 1024
FFN_SUB_ROWS = 256


def _params(semantics, vmem_mib):
    assert vmem_mib * MIB < V7X_VMEM_BYTES
    return pltpu.CompilerParams(dimension_semantics=semantics, vmem_limit_bytes=vmem_mib * MIB)


def _tile(n, target):
    t = 1
    while t * 2 <= target and n % (t * 2) == 0:
        t *= 2
    return t


def _rope_table_kernel(cos_ref, sin_ref):
    rows, hd = cos_ref.shape
    half = hd // 2
    lane = lax.broadcasted_iota(jnp.int32, (rows, hd), 1)
    j = jnp.where(lane < half, lane, lane - half).astype(F32)
    inv = jnp.power(jnp.float32(ROPE_THETA), -j / half)
    pos = (lax.broadcasted_iota(jnp.int32, (rows, hd), 0) + pl.program_id(0) * rows).astype(F32)
    ang = pos * inv
    cos_ref[...] = jnp.cos(ang)
    s = jnp.sin(ang)
    sin_ref[...] = jnp.where(lane < half, -s, s)


def _rope_tables(n_pos, head_dim):
    rows = _tile(n_pos, 512)
    return pl.pallas_call(
        _rope_table_kernel,
        grid=(n_pos // rows,),
        out_specs=[pl.BlockSpec((rows, head_dim), lambda i: (i, 0))] * 2,
        out_shape=[jax.ShapeDtypeStruct((n_pos, head_dim), F32)] * 2,
        compiler_params=_params(("arbitrary",), 16),
        name="rope_tables",
    )()


def _rms(xf, g):
    return xf * lax.rsqrt(jnp.mean(xf * xf, axis=-1, keepdims=True) + EPS) * g


def _addnorm_kernel(*refs, has_y, emit_x):
    it = iter(refs)
    x_ref = next(it)
    y_ref = next(it) if has_y else None
    g_ref = next(it)
    xo_ref = next(it) if emit_x else None
    n_ref = next(it)
    x = x_ref[...]
    if has_y:
        x = x + y_ref[...]
    if emit_x:
        xo_ref[...] = x
    n_ref[...] = _rms(x, g_ref[...]).astype(n_ref.dtype)


def _addnorm(x, y, g, out_dtype):
    m, d = x.shape
    tm = _tile(m, 256)
    has_y = y is not None
    row = pl.BlockSpec((tm, d), lambda i: (i, 0))
    ins = [x] + ([y] if has_y else []) + [g.reshape(1, d)]
    in_specs = [row] * (2 if has_y else 1) + [pl.BlockSpec((1, d), lambda i: (0, 0))]
    out_shape = ([jax.ShapeDtypeStruct((m, d), F32)] if has_y else []) + [jax.ShapeDtypeStruct((m, d), out_dtype)]
    outs = pl.pallas_call(
        functools.partial(_addnorm_kernel, has_y=has_y, emit_x=has_y),
        grid=(m // tm,),
        in_specs=in_specs,
        out_specs=[row] * len(out_shape),
        out_shape=out_shape,
        compiler_params=_params(("arbitrary",), 48),
        name="add_rmsnorm",
    )(*ins)
    return (outs[0], outs[1]) if has_y else (x, outs[0])


def _final_norm_kernel(*refs, has_y, prompt_tiles):
    it = iter(refs)
    x_ref = next(it)
    y_ref = next(it) if has_y else None
    g_ref, p_ref, s_ref = next(it), next(it), next(it)
    i = pl.program_id(0)
    x = x_ref[...]
    if has_y:
        x = x + y_ref[...]
    n = _rms(x, g_ref[...])

    @pl.when(i < prompt_tiles)
    def _():
        p_ref[...] = n

    @pl.when(i >= prompt_tiles)
    def _():
        s_ref[...] = n


def _final_norm(x, y, g, mp):
    m, d = x.shape
    ms = m - mp
    tm = min(_tile(mp, 256), _tile(ms, 256))
    pt = mp // tm
    has_y = y is not None
    row = pl.BlockSpec((tm, d), lambda i: (i, 0))
    return pl.pallas_call(
        functools.partial(_final_norm_kernel, has_y=has_y, prompt_tiles=pt),
        grid=(m // tm,),
        in_specs=[row] * (2 if has_y else 1) + [pl.BlockSpec((1, d), lambda i: (0, 0))],
        out_specs=[pl.BlockSpec((tm, d), lambda i: (jnp.minimum(i, pt - 1), 0)),
                   pl.BlockSpec((tm, d), lambda i: (jnp.maximum(i - pt, 0), 0))],
        out_shape=[jax.ShapeDtypeStruct((mp, d), F32), jax.ShapeDtypeStruct((ms, d), F32)],
        compiler_params=_params(("arbitrary",), 48),
        name="final_rmsnorm",
    )(*([x] + ([y] if has_y else []) + [g.reshape(1, d)]))


def _mm_kernel(x_ref, w_ref, o_ref, wbf_ref):
    @pl.when(pl.program_id(1) == 0)
    def _():
        wbf_ref[...] = w_ref[...].astype(BF16)

    o_ref[...] = jnp.dot(x_ref[...], wbf_ref[...], preferred_element_type=F32).astype(o_ref.dtype)


def _mm(x, w, layer, col0, n, out_dtype=F32):
    m, k = x.shape
    tm = _tile(m, 1024)
    tn = _tile(n, 512)
    assert col0 % tn == 0 and tn % LANES == 0
    off = col0 // tn
    return pl.pallas_call(
        _mm_kernel,
        grid=(n // tn, m // tm),
        in_specs=[
            pl.BlockSpec((tm, k), lambda j, i: (i, 0)),
            pl.BlockSpec((None, k, tn), lambda j, i: (layer, 0, j + off)),
        ],
        out_specs=pl.BlockSpec((tm, tn), lambda j, i: (i, j)),
        out_shape=jax.ShapeDtypeStruct((m, n), out_dtype),
        scratch_shapes=[pltpu.VMEM((k, tn), BF16)],
        compiler_params=_params(("arbitrary", "arbitrary"), 56),
        name="matmul",
    )(x, w)


def _log_sigmoid(z):
    y = -z
    return -(jnp.maximum(y, 0.0) + jnp.log1p(jnp.exp(-jnp.abs(y))))


def _lowrank_kernel(n_ref, w1_ref, w2_ref, b_ref, o_ref):
    glr = jnp.dot(n_ref[...], w1_ref[...], preferred_element_type=F32)
    z = jnp.dot(glr.astype(BF16), w2_ref[...], preferred_element_type=F32) + b_ref[...]
    o_ref[...] = _log_sigmoid(z) / GLA_TAU


def _lowrank_gate(n, w1, w2, b):
    m, d = n.shape
    gqk = w2.shape[1]
    tm = _tile(m, 512)
    return pl.pallas_call(
        _lowrank_kernel,
        grid=(m // tm,),
        in_specs=[
            pl.BlockSpec((tm, d), lambda i: (i, 0)),
            pl.BlockSpec(w1.shape, lambda i: (0, 0)),
            pl.BlockSpec(w2.shape, lambda i: (0, 0)),
            pl.BlockSpec((1, gqk), lambda i: (0, 0)),
        ],
        out_specs=pl.BlockSpec((tm, gqk), lambda i: (i, 0)),
        out_shape=jax.ShapeDtypeStruct((m, gqk), F32),
        compiler_params=_params(("arbitrary",), 32),
        name="gla_forget_gate",
    )(n, w1, w2, b)


def _rope(x, cos, sin):
    return x * cos + pltpu.roll(x, x.shape[-1] // 2, 1) * sin


def _swa_kernel(sink_ref, q_ref, k0_ref, k1_ref, k2_ref, v0_ref, v1_ref, v2_ref,
                c0_ref, c1_ref, c2_ref, s0_ref, s1_ref, s2_ref, o_ref, krot_ref, *, groups, cached):
    c = pl.program_id(1)
    hd = c2_ref.shape[-1]
    n_kv = k2_ref.shape[-1] // hd
    cosq, sinq = c2_ref[...], s2_ref[...]
    for kvh in range(n_kv):
        ks = slice(kvh * hd, (kvh + 1) * hd)
        k2 = _rope(k2_ref[:, ks], cosq, sinq)
        krot_ref[:, ks] = k2
        if cached:
            k0, k1 = k0_ref[:, ks], k1_ref[:, ks]
        else:
            k0 = _rope(k0_ref[:, ks], c0_ref[...], s0_ref[...])
            k1 = _rope(k1_ref[:, ks], c1_ref[...], s1_ref[...])
        kall = jnp.concatenate([k0, k1, k2], axis=0).astype(BF16)
        vall = jnp.concatenate([v0_ref[:, ks], v1_ref[:, ks], v2_ref[:, ks]], axis=0).astype(BF16)
        q0 = kvh * groups * hd
        qall = jnp.concatenate(
            [_rope(q_ref[:, q0 + g * hd:q0 + (g + 1) * hd], cosq, sinq) for g in range(groups)], axis=0).astype(BF16)
        s = lax.dot_general(qall, kall, (((1,), (1,)), ((), ())), preferred_element_type=F32) * hd ** -0.5
        if not cached:
            col = lax.broadcasted_iota(jnp.int32, s.shape, 1)
            s = jnp.where(col >= jnp.maximum(2 - c, 0) * CHUNK, s, NEG_INF)
        sink = jnp.concatenate(
            [jnp.full((CHUNK, 1), sink_ref[kvh * groups + g], F32) for g in range(groups)], axis=0)
        mx = jnp.maximum(jnp.max(s, axis=-1, keepdims=True), sink)
        p = jnp.exp(s - mx)
        den = jnp.sum(p, axis=-1, keepdims=True) + jnp.exp(sink - mx)
        p = p / den
        o = jnp.dot(p.astype(BF16), vall, preferred_element_type=F32)
        for g in range(groups):
            o_ref[:, q0 + g * hd:q0 + (g + 1) * hd] = o[g * CHUNK:(g + 1) * CHUNK].astype(o_ref.dtype)


def _swa(u, sink, cos, sin, cache_k, cache_v, *, n_seq, n_chunks, row0, n_kv, groups, hd, pos0, aq):
    cached = cache_k is not None
    akv = n_kv * hd
    assert aq % akv == 0
    rb0 = row0 // CHUNK
    pb0 = pos0 // CHUNK
    kcol, vcol = aq // akv, aq // akv + 1

    def rowblk(b, c):
        return rb0 + b * n_chunks + c

    def window(colblk):
        return [pl.BlockSpec((CHUNK, akv), lambda b, c, j=j: (rowblk(b, jnp.maximum(c - 2 + j, 0)), colblk))
                for j in range(2)]

    if cached:
        kprev = vprev = [pl.BlockSpec((CHUNK, akv), lambda b, c, j=j: (2 * b + j, 0)) for j in range(2)]
        kin, vin = [cache_k, cache_k], [cache_v, cache_v]
    else:
        kprev, vprev = window(kcol), window(vcol)
        kin = vin = [u, u]
    q_spec = pl.BlockSpec((CHUNK, aq), lambda b, c: (rowblk(b, c), 0))
    kcur = pl.BlockSpec((CHUNK, akv), lambda b, c: (rowblk(b, c), kcol))
    vcur = pl.BlockSpec((CHUNK, akv), lambda b, c: (rowblk(b, c), vcol))
    tab = [pl.BlockSpec((CHUNK, hd), lambda b, c, j=j: (pb0 + jnp.maximum(c - 2 + j, 0), 0)) for j in range(3)]
    out_rows = n_seq * n_chunks * CHUNK
    return pl.pallas_call(
        functools.partial(_swa_kernel, groups=groups, cached=cached),
        grid=(n_seq, n_chunks),
        in_specs=[pl.BlockSpec(memory_space=pltpu.SMEM), q_spec] + kprev + [kcur] + vprev + [vcur] + tab + tab,
        out_specs=[pl.BlockSpec((CHUNK, aq), lambda b, c: (b * n_chunks + c, 0)),
                   pl.BlockSpec((CHUNK, akv), lambda b, c: (b * n_chunks + c, 0))],
        out_shape=[jax.ShapeDtypeStruct((out_rows, aq), BF16), jax.ShapeDtypeStruct((out_rows, akv), F32)],
        compiler_params=_params(("arbitrary",) * 2, 32),
        name="swa_cached" if cached else "swa_prompt",
    )(sink, u, *kin, u, *vin, u, cos, cos, cos, sin, sin, sin)


def _cumsum_rows(x):
    row = lax.broadcasted_iota(jnp.int32, x.shape, 0)
    d = 1
    while d < x.shape[0]:
        x = x + jnp.where(row >= d, pltpu.roll(x, d, 0), 0.0)
        d *= 2
    return x


def _gla_kernel(*refs, heads, has_init):
    it = iter(refs)
    q_ref, k_ref, lg_ref, g_ref = (next(it) for _ in range(4))
    v_refs = [next(it) for _ in range(heads)]
    r_refs = [next(it) for _ in range(heads)]
    s0_ref = next(it) if has_init else None
    o_ref, sout_ref, st_ref = next(it), next(it), next(it)
    c = pl.program_id(1)
    dk = q_ref.shape[-1] // heads
    dv = v_refs[0].shape[-1]

    @pl.when(c == 0)
    def _():
        for h in range(heads):
            st_ref[h] = s0_ref[h].T if has_init else jnp.zeros((dv, dk), F32)

    tri = lax.broadcasted_iota(jnp.int32, (CHUNK, CHUNK), 0) >= lax.broadcasted_iota(jnp.int32, (CHUNK, CHUNK), 1)
    for h in range(heads):
        ks = slice(h * dk, (h + 1) * dk)
        b = _cumsum_rows(lg_ref[:, ks])
        b_last = b[CHUNK - 1:CHUNK, :]
        k = k_ref[:, ks]
        qe = (q_ref[:, ks] * dk ** -0.5 * jnp.exp(b)).astype(BF16)
        ke = (k * jnp.exp(-b)).astype(BF16)
        kd = (k * jnp.exp(b_last - b)).astype(BF16)
        v = v_refs[h][...].astype(BF16)
        att = lax.dot_general(qe, ke, (((1,), (1,)), ((), ())), preferred_element_type=F32)
        att = jnp.where(tri, att, 0.0)
        st = st_ref[h]
        o = lax.dot_general(qe, st.astype(BF16), (((1,), (1,)), ((), ())), preferred_element_type=F32)
        o = o + jnp.dot(att.astype(BF16), v, preferred_element_type=F32)
        st_new = jnp.exp(b_last) * st + lax.dot_general(v, kd, (((0,), (0,)), ((), ())), preferred_element_type=F32)
        st_ref[h] = st_new
        r = r_refs[h][...]
        o_ref[:, h * dv:(h + 1) * dv] = (_rms(o, g_ref[...]) * (r * jax.nn.sigmoid(r))).astype(o_ref.dtype)

    @pl.when(c == pl.num_programs(1) - 1)
    def _():
        for h in range(heads):
            sout_ref[h] = st_ref[h].T


def _gla(u, lg, g_norm, s0, *, n_seq, n_chunks, row0, heads, dk, dv, q_col, k_col, v_col, r_col):
    has_init = s0 is not None
    rb0 = row0 // CHUNK
    gqk = heads * dk
    assert q_col % gqk == 0 and k_col % gqk == 0 and v_col % dv == 0 and r_col % dv == 0

    def col(width, blk):
        return pl.BlockSpec((CHUNK, width), lambda b, c: (rb0 + b * n_chunks + c, blk))

    in_specs = ([col(gqk, q_col // gqk), col(gqk, k_col // gqk), col(gqk, 0), pl.BlockSpec((1, dv), lambda b, c: (0, 0))]
                + [col(dv, v_col // dv + h) for h in range(heads)] + [col(dv, r_col // dv + h) for h in range(heads)])
    ins = [u, u, lg, g_norm.reshape(1, dv)] + [u] * (2 * heads)
    if has_init:
        in_specs.append(pl.BlockSpec((None, heads, dk, dv), lambda b, c: (b, 0, 0, 0)))
        ins.append(s0)
    out_rows = n_seq * n_chunks * CHUNK
    return pl.pallas_call(
        functools.partial(_gla_kernel, heads=heads, has_init=has_init),
        grid=(n_seq, n_chunks),
        in_specs=in_specs,
        out_specs=[pl.BlockSpec((CHUNK, heads * dv), lambda b, c: (b * n_chunks + c, 0)),
                   pl.BlockSpec((None, heads, dk, dv), lambda b, c: (b, 0, 0, 0))],
        out_shape=[jax.ShapeDtypeStruct((out_rows, heads * dv), BF16),
                   jax.ShapeDtypeStruct((n_seq, heads, dk, dv), F32)],
        scratch_shapes=[pltpu.VMEM((heads, dv, dk), F32)],
        compiler_params=_params(("arbitrary",) * 2, 40),
        name="gla_init" if has_init else "gla_prompt",
    )(*ins)


def _softplus(y):
    return jnp.maximum(y, 0.0) + jnp.log1p(jnp.exp(-jnp.abs(y)))


def _lru_kernel(x_ref, xprev_ref, cprev_ref, y_ref, wc_ref, bc_ref, wa_ref, ba_ref, wx_ref, bx_ref, lam_ref, h0_ref,
                o_ref, hl_ref, h_ref, a_ref, u_ref, *, stream_start, conv_w):
    c = pl.program_id(1)
    t, w = x_ref.shape
    nblk, bw = wa_ref.shape[0], wa_ref.shape[1]

    @pl.when(c == 0)
    def _():
        h_ref[...] = h0_ref[...]

    prev = jnp.where(c == 0, cprev_ref[...], xprev_ref[...])
    ext = jnp.concatenate([prev, x_ref[...]], axis=0)
    xc = bc_ref[...]
    for j in range(conv_w):
        shift = conv_w - 1 - j
        xs = ext if shift == 0 else pltpu.roll(ext, shift, 0)
        xc = xc + xs[SUBLANES:] * wc_ref[j:j + 1, :]
    xb = xc.astype(BF16)
    ra = jnp.concatenate([jnp.dot(xb[:, n * bw:(n + 1) * bw], wa_ref[n], preferred_element_type=F32)
                          for n in range(nblk)], axis=1)
    ia = jnp.concatenate([jnp.dot(xb[:, n * bw:(n + 1) * bw], wx_ref[n], preferred_element_type=F32)
                          for n in range(nblk)], axis=1)
    r = jax.nn.sigmoid(ra + ba_ref[...])
    gate = jax.nn.sigmoid(ia + bx_ref[...])
    log_a = -LRU_C * r * _softplus(-lam_ref[...])
    a = jnp.exp(log_a)
    mult = jnp.sqrt(jnp.tanh(-log_a) * (a * a + 1.0))
    if stream_start:
        first = (lax.broadcasted_iota(jnp.int32, (t, w), 0) == 0) & (c == 0)
        mult = jnp.where(first, 1.0, mult)
        a = jnp.where(first, 0.0, a)
    a_ref[...] = a
    u_ref[...] = mult * gate * xc

    def step(i, h):
        h = a_ref[pl.ds(i, 1), :] * h + u_ref[pl.ds(i, 1), :]
        u_ref[pl.ds(i, 1), :] = h
        return h

    h = lax.fori_loop(0, t, step, h_ref[...], unroll=8)
    h_ref[...] = h
    hl_ref[...] = h
    o_ref[...] = (jax.nn.gelu(y_ref[...]) * u_ref[...]).astype(o_ref.dtype)


def _lru(uc, cprev8, h0, wc, bc, wa, ba, wx, bx, lam, *, n_seq, n_chunks, row0, width, stream_start):
    rb0 = row0 // CHUNK
    per = CHUNK // SUBLANES
    conv_w = wc.shape[0]
    vec = pl.BlockSpec((1, width), lambda b, c: (0, 0))
    blk = pl.BlockSpec(wa.shape, lambda b, c: (0, 0, 0))
    out_rows = n_seq * n_chunks * CHUNK
    return pl.pallas_call(
        functools.partial(_lru_kernel, stream_start=stream_start, conv_w=conv_w),
        grid=(n_seq, n_chunks),
        in_specs=[
            pl.BlockSpec((CHUNK, width), lambda b, c: (rb0 + b * n_chunks + c, 0)),
            pl.BlockSpec((SUBLANES, width), lambda b, c: (jnp.maximum((rb0 + b * n_chunks + c) * per - 1, 0), 0)),
            pl.BlockSpec((None, SUBLANES, width), lambda b, c: (b, 0, 0)),
            pl.BlockSpec((CHUNK, width), lambda b, c: (rb0 + b * n_chunks + c, 1)),
            pl.BlockSpec(wc.shape, lambda b, c: (0, 0)), vec, blk, vec, blk, vec, vec,
            pl.BlockSpec((None, 1, width), lambda b, c: (b, 0, 0)),
        ],
        out_specs=[pl.BlockSpec((CHUNK, width), lambda b, c: (b * n_chunks + c, 0)),
                   pl.BlockSpec((None, 1, width), lambda b, c: (b, 0, 0))],
        out_shape=[jax.ShapeDtypeStruct((out_rows, width), BF16), jax.ShapeDtypeStruct((n_seq, 1, width), F32)],
        scratch_shapes=[pltpu.VMEM((1, width), F32), pltpu.VMEM((CHUNK, width), F32), pltpu.VMEM((CHUNK, width), F32)],
        compiler_params=_params(("arbitrary",) * 2, 32),
        name="conv_rglru",
    )(uc, uc, cprev8, uc, wc, bc, wa, ba, wx, bx, lam, h0)


def _merge_kernel(n_ref, ap_ref, bp_ref, cp_ref, as_ref, bs_ref, cs_ref, wg0_ref, wg1_ref, wg2_ref, wb_ref,
                  bg0_ref, bg1_ref, bg2_ref, o_ref, wgbf_ref, wbbf_ref, *, prompt_tiles):
    i = pl.program_id(1)

    @pl.when(i == 0)
    def _():
        for b, wg_ref in enumerate((wg0_ref, wg1_ref, wg2_ref)):
            wgbf_ref[b] = wg_ref[...].astype(BF16)
            wbbf_ref[b] = wb_ref[b].astype(BF16)

    n = n_ref[...]
    acc = None
    branches = ((ap_ref, as_ref, bg0_ref), (bp_ref, bs_ref, bg1_ref), (cp_ref, cs_ref, bg2_ref))
    for b, (p_ref, s_ref, bg_ref) in enumerate(branches):
        br = jnp.where(i < prompt_tiles, p_ref[...], s_ref[...])
        gate = jax.nn.sigmoid(jnp.dot(n, wgbf_ref[b], preferred_element_type=F32) + bg_ref[...])
        term = gate * jnp.dot(br, wbbf_ref[b], preferred_element_type=F32)
        acc = term if acc is None else acc + term
    o_ref[...] = acc.astype(o_ref.dtype)


def _merge(n, prompt, sample, w_gate, b_gate, w_branch, layer):
    m, d = n.shape
    mp, bw = prompt[0].shape
    ms = sample[0].shape[0]
    nb = w_branch.shape[1]
    assert nb == 3 and mp + ms == m
    tm = min(_tile(mp, 512), _tile(ms, 512))
    tn = _tile(d, 256)
    nj = d // tn
    pt = mp // tm
    single = pl.Buffered(1)
    wg_specs = [pl.BlockSpec((None, d, tn), lambda j, i, b=b: (layer, 0, b * nj + j), pipeline_mode=single)
                for b in range(nb)]
    bg_specs = [pl.BlockSpec((None, 1, tn), lambda j, i, b=b: (layer, 0, b * nj + j)) for b in range(nb)]
    p_spec = pl.BlockSpec((tm, bw), lambda j, i: (jnp.minimum(i, pt - 1), 0))
    s_spec = pl.BlockSpec((tm, bw), lambda j, i: (jnp.maximum(i - pt, 0), 0), pipeline_mode=single)
    return pl.pallas_call(
        functools.partial(_merge_kernel, prompt_tiles=pt),
        grid=(nj, m // tm),
        in_specs=[pl.BlockSpec((tm, d), lambda j, i: (i, 0))] + [p_spec] * nb + [s_spec] * nb + wg_specs
        + [pl.BlockSpec((None, nb, bw, tn), lambda j, i: (layer, 0, 0, j), pipeline_mode=single)] + bg_specs,
        out_specs=pl.BlockSpec((tm, tn), lambda j, i: (i, j)),
        out_shape=jax.ShapeDtypeStruct((m, d), BF16),
        scratch_shapes=[pltpu.VMEM((nb, d, tn), BF16), pltpu.VMEM((nb, bw, tn), BF16)],
        compiler_params=_params(("arbitrary", "arbitrary"), 60),
        name="gated_merge",
    )(n, *prompt, *sample, w_gate, w_gate, w_gate, w_branch, b_gate, b_gate, b_gate)


def _ffn_kernel(te_ref, nv_ref, ns_ref, x_ref, wg_ref, wu_ref, wd_ref, o_ref, *bf_refs, sub):
    i, j = pl.program_id(0), pl.program_id(1)
    tm, d = o_ref.shape
    dc = min(d, FFN_DOWN_COLS)
    nsub = tm // sub
    used = ns_ref[i]

    def rows_body(rows, wg, wu, wd_cols):
        x = x_ref[rows, :]
        g = jnp.dot(x, wg, preferred_element_type=F32)
        u = jnp.dot(x, wu, preferred_element_type=F32)
        h = (g * jax.nn.sigmoid(g) * u).astype(BF16)

        @pl.when(j == 0)
        def _():
            for n0 in range(0, d, dc):
                o_ref[rows, n0:n0 + dc] = jnp.dot(h, wd_cols(n0), preferred_element_type=F32)

        @pl.when(j > 0)
        def _():
            for n0 in range(0, d, dc):
                o_ref[rows, n0:n0 + dc] += jnp.dot(h, wd_cols(n0), preferred_element_type=F32)

    @pl.when(used == nsub)
    def _():
        rows_body(slice(0, tm), wg_ref[...].astype(BF16), wu_ref[...].astype(BF16),
                  lambda n0: wd_ref[:, n0:n0 + dc].astype(BF16))

    if nsub > 1:
        wgbf_ref, wubf_ref, wdbf_ref = bf_refs

        @pl.when((used > 0) & (used < nsub))
        def _():
            wg, wu, wd = wg_ref[...].astype(BF16), wu_ref[...].astype(BF16), wd_ref[...].astype(BF16)
            wgbf_ref[...] = wg
            wubf_ref[...] = wu
            wdbf_ref[...] = wd
            rows_body(slice(0, sub), wg, wu, lambda n0: wd[:, n0:n0 + dc])
            for s in range(1, nsub - 1):
                @pl.when(s < used)
                def _(s=s):
                    rows_body(slice(s * sub, (s + 1) * sub), wgbf_ref[...], wubf_ref[...],
                              lambda n0: wdbf_ref[:, n0:n0 + dc])

    for s in range(nsub):
        @pl.when((s >= used) & (j == 0))
        def _(s=s):
            o_ref[s * sub:(s + 1) * sub, :] = jnp.zeros((sub, d), o_ref.dtype)


def _ffn(x, wg, wu, wd, tile_expert, n_valid, n_sub, tm, sub):
    r, d = x.shape
    f = wg.shape[2]
    tf = _tile(f, 256)
    nj = f // tf
    single = pl.Buffered(1)

    def last(i, nv):
        return jnp.maximum(jnp.minimum(i, nv[0] - 1), 0)

    def row(i, j, te, nv, ns):
        return (last(i, nv), 0)

    def fidx(i, j, nv):
        return jnp.where(i < nv[0], j, nj - 1)

    def up(i, j, te, nv, ns):
        return (te[last(i, nv)], 0, fidx(i, j, nv))

    def down(i, j, te, nv, ns):
        return (te[last(i, nv)], fidx(i, j, nv), 0)

    return pl.pallas_call(
        functools.partial(_ffn_kernel, sub=sub),
        grid_spec=pltpu.PrefetchScalarGridSpec(
            num_scalar_prefetch=3,
            grid=(r // tm, nj),
            in_specs=[pl.BlockSpec((tm, d), row, pipeline_mode=single),
                      pl.BlockSpec((None, d, tf), up), pl.BlockSpec((None, d, tf), up),
                      pl.BlockSpec((None, tf, d), down)],
            out_specs=pl.BlockSpec((tm, d), lambda i, j, te, nv, ns: (i, 0), pipeline_mode=single),
            scratch_shapes=([pltpu.VMEM((d, tf), BF16), pltpu.VMEM((d, tf), BF16), pltpu.VMEM((tf, d), BF16)]
                            if sub < tm else []),
        ),
        out_shape=jax.ShapeDtypeStruct((r, d), F32),
        compiler_params=_params(("arbitrary", "arbitrary"), 60),
        name="swiglu_ffn",
    )(tile_expert, n_valid, n_sub, x, wg, wu, wd)


def _router_kernel(x_ref, g_ref, w_ref, meta_ref, cnt_ref, carry_ref, *, n_experts):
    @pl.when(pl.program_id(0) == 0)
    def _():
        carry_ref[...] = jnp.zeros_like(carry_ref)

    tm = x_ref.shape[0]
    n = _rms(x_ref[...], g_ref[...])
    logits = jnp.dot(n, w_ref[...], preferred_element_type=F32, precision=lax.Precision.HIGHEST)
    lane = lax.broadcasted_iota(jnp.int32, logits.shape, 1).astype(F32)
    logits = jnp.where(lane < n_experts, logits, -jnp.inf)
    m1 = jnp.max(logits, axis=-1, keepdims=True)
    i1 = jnp.min(jnp.where(logits == m1, lane, float(LANES)), axis=-1, keepdims=True)
    rest = jnp.where(lane == i1, -jnp.inf, logits)
    m2 = jnp.max(rest, axis=-1, keepdims=True)
    i2 = jnp.min(jnp.where(rest == m2, lane, float(LANES)), axis=-1, keepdims=True)
    e2 = jnp.exp(m2 - m1)
    den = 1.0 + e2
    oh1 = lane == i1
    oh2 = lane == i2
    both = jnp.where(oh1 | oh2, 1.0, 0.0)
    rr = lax.broadcasted_iota(jnp.int32, (tm, tm), 0)
    cc = lax.broadcasted_iota(jnp.int32, (tm, tm), 1)
    below = jnp.where(rr > cc, 1.0, 0.0).astype(BF16)
    before = jnp.dot(below, both.astype(BF16), preferred_element_type=F32) + carry_ref[...]
    rank1 = jnp.sum(jnp.where(oh1, before, 0.0), axis=-1, keepdims=True)
    rank2 = jnp.sum(jnp.where(oh2, before, 0.0), axis=-1, keepdims=True)
    cols = (i1, i2, 1.0 / den, e2 / den, rank1, rank2)
    meta = jnp.zeros(logits.shape, F32)
    for idx, val in enumerate(cols):
        meta = jnp.where(lane == idx, val, meta)
    meta_ref[...] = meta
    total = carry_ref[...] + jnp.sum(both, axis=0, keepdims=True)
    carry_ref[...] = total
    cnt_ref[...] = total


def _router(x, g, w_pad, n_experts):
    m, d = x.shape
    tm = _tile(m, 256)
    return pl.pallas_call(
        functools.partial(_router_kernel, n_experts=n_experts),
        grid=(m // tm,),
        in_specs=[pl.BlockSpec((tm, d), lambda i: (i, 0)), pl.BlockSpec((1, d), lambda i: (0, 0)),
                  pl.BlockSpec(w_pad.shape, lambda i: (0, 0))],
        out_specs=[pl.BlockSpec((tm, LANES), lambda i: (i, 0)), pl.BlockSpec((1, LANES), lambda i: (0, 0))],
        out_shape=[jax.ShapeDtypeStruct((m, LANES), F32), jax.ShapeDtypeStruct((1, LANES), F32)],
        scratch_shapes=[pltpu.VMEM((1, LANES), F32)],
        compiler_params=_params(("arbitrary",), 40),
        name="moe_router",
    )(x, g.reshape(1, d), w_pad)


def _dispatch_kernel(used_ref, src_ref, nxt_ref, x_hbm, g_ref, o_ref, buf_ref, sem):
    i = pl.program_id(0)
    rows = o_ref.shape[0]
    slot = i % 2

    def copy(tbl_ref, r, s):
        return pltpu.make_async_copy(x_hbm.at[pl.ds(tbl_ref[0, r], 1)], buf_ref.at[s, pl.ds(r, 1)], sem.at[s])

    def start_all(tbl_ref, s):
        def body(r, _):
            copy(tbl_ref, r, s).start()
            return 0
        lax.fori_loop(0, rows, body, 0)

    def in_use(step):
        return step * rows < used_ref[0]

    @pl.when((i == 0) & in_use(0))
    def _():
        start_all(src_ref, 0)

    @pl.when((i + 1 < pl.num_programs(0)) & in_use(i + 1))
    def _():
        start_all(nxt_ref, 1 - slot)

    @pl.when(in_use(i))
    def _():
        def body(r, _):
            copy(src_ref, r, slot).wait()
            return 0
        lax.fori_loop(0, rows, body, 0)
        o_ref[...] = _rms(buf_ref[slot], g_ref[...]).astype(o_ref.dtype)

    @pl.when(jnp.logical_not(in_use(i)))
    def _():
        o_ref[...] = jnp.zeros_like(o_ref)


def _dispatch(x, g, src, n_used):
    m, d = x.shape
    n_rows = src.shape[0]
    rows = _tile(n_rows, 256)
    steps = n_rows // rows
    src3 = src.reshape(steps, 1, rows)
    return pl.pallas_call(
        _dispatch_kernel,
        grid=(steps,),
        in_specs=[pl.BlockSpec(memory_space=pltpu.SMEM),
                  pl.BlockSpec((None, 1, rows), lambda i: (i, 0, 0), memory_space=pltpu.SMEM),
                  pl.BlockSpec((None, 1, rows), lambda i: (jnp.minimum(i + 1, steps - 1), 0, 0),
                               memory_space=pltpu.SMEM),
                  pl.BlockSpec(memory_space=pl.ANY), pl.BlockSpec((1, d), lambda i: (0, 0))],
        out_specs=pl.BlockSpec((rows, d), lambda i: (i, 0)),
        out_shape=jax.ShapeDtypeStruct((n_rows, d), BF16),
        scratch_shapes=[pltpu.VMEM((2, rows, d), F32), pltpu.SemaphoreType.DMA((2,))],
        compiler_params=_params(("arbitrary",), 32),
        name="moe_dispatch",
    )(n_used, src3, src3, x, g.reshape(1, d))


def _combine_kernel(dest_ref, x_ref, meta_ref, y_hbm, o_ref, ya_ref, yb_ref, sem, *, rows):
    bufs = (ya_ref, yb_ref)

    def copy(r, k):
        return pltpu.make_async_copy(y_hbm.at[pl.ds(dest_ref[0, TOP_K * r + k], 1)], bufs[k].at[pl.ds(r, 1)], sem)

    def start(r, _):
        for k in range(TOP_K):
            copy(r, k).start()
        return 0

    def wait(r, _):
        for k in range(TOP_K):
            copy(r, k).wait()
        return 0

    lax.fori_loop(0, rows, start, 0)
    lax.fori_loop(0, rows, wait, 0)
    meta = meta_ref[...]
    o_ref[...] = x_ref[...] + (meta[:, 2:3] * ya_ref[...] + meta[:, 3:4] * yb_ref[...])


def _combine(x, meta, y, dest):
    m, d = x.shape
    rows = _tile(m, 128)
    dest3 = dest.reshape(m // rows, 1, rows * TOP_K)
    return pl.pallas_call(
        functools.partial(_combine_kernel, rows=rows),
        grid=(m // rows,),
        in_specs=[pl.BlockSpec((None, 1, rows * TOP_K), lambda i: (i, 0, 0), memory_space=pltpu.SMEM),
                  pl.BlockSpec((rows, d), lambda i: (i, 0)), pl.BlockSpec((rows, LANES), lambda i: (i, 0)),
                  pl.BlockSpec(memory_space=pl.ANY)],
        out_specs=pl.BlockSpec((rows, d), lambda i: (i, 0)),
        out_shape=jax.ShapeDtypeStruct((m, d), F32),
        scratch_shapes=[pltpu.VMEM((rows, d), F32), pltpu.VMEM((rows, d), F32), pltpu.SemaphoreType.DMA(())],
        compiler_params=_params(("arbitrary",), 32),
        name="moe_combine",
    )(dest3, x, meta, y)


def _moe(x, g, w_router, wg, wu, wd, tm):
    m, d = x.shape
    n_exp = w_router.shape[1]
    w_pad = jnp.zeros((d, LANES), F32).at[:, :n_exp].set(w_router)
    meta, counts = _router(x, g, w_pad, n_exp)
    expert = meta[:, 0:TOP_K].astype(jnp.int32)
    rank = meta[:, 4:4 + TOP_K].astype(jnp.int32)
    cnt = counts[0, :n_exp].astype(jnp.int32)
    padded = (cnt + tm - 1) // tm * tm
    ends = jnp.cumsum(padded)
    dest = (ends - padded)[expert] + rank
    n_tiles = (m * TOP_K) // tm + n_exp
    tile_ids = jnp.arange(n_tiles, dtype=jnp.int32)
    tile_expert = jnp.minimum(jnp.sum((ends // tm)[None, :] <= tile_ids[:, None], axis=1), n_exp - 1).astype(jnp.int32)
    n_valid = (ends[-1:] // tm).astype(jnp.int32)
    sub = FFN_SUB_ROWS if tm % FFN_SUB_ROWS == 0 else tm
    used = jnp.clip(cnt[tile_expert] - (tile_ids - ((ends - padded) // tm)[tile_expert]) * tm, 0, tm)
    n_sub = jnp.where(tile_ids < n_valid[0], (used + sub - 1) // sub, 0).astype(jnp.int32)
    token = jnp.broadcast_to(jnp.arange(m, dtype=jnp.int32)[:, None], (m, TOP_K))
    src = jnp.zeros((n_tiles * tm,), jnp.int32).at[dest.reshape(-1)].set(token.reshape(-1))
    ns = _dispatch(x, g, src, ends[-1:].astype(jnp.int32))
    ys = _ffn(ns, wg, wu, wd, tile_expert, n_valid, n_sub, tm, sub)
    return _combine(x, meta, ys, dest)


def kernel(x_prompt, x_sample, cache_swa_k, cache_swa_v, state_gla, state_conv, state_lru, ln_mix, w_in, attn_sink, w_gla_lr2, b_gla_gate, g_gla_norm, w_conv, b_conv, w_lru_a, b_lru_a, w_lru_x, b_lru_x, lru_lambda, w_branch, w_gate, b_gate, w_out, ln_ffn, w_ff_gate, w_ff_up, w_ff_down, w_router, w_moe_gate, w_moe_up, w_moe_down, ln_final):
    bp, seq, d = x_prompt.shape
    bs, dseq, _ = x_sample.shape
    depth = ln_mix.shape[0]
    _, _, window, n_kv, hd = cache_swa_k.shape
    n_heads = attn_sink.shape[1]
    groups = n_heads // n_kv
    _, _, gh, dk, dv = state_gla.shape
    lowrank = w_gla_lr2.shape[1]
    conv_w = w_conv.shape[1]
    width = state_lru.shape[2]
    aq, akv, gqk, gv = n_heads * hd, n_kv * hd, gh * dk, gh * dv
    assert dseq == CHUNK and seq % CHUNK == 0 and window == 2 * CHUNK and hd == LANES and lowrank <= LANES
    mp, ms = bp * seq, bs * dseq
    m = mp + ms
    nc = seq // CHUNK
    main = aq + 2 * akv + 2 * gqk + 2 * gv
    c_col = main + lowrank
    assert w_in.shape[2] == c_col + 2 * width

    w_lr = jnp.zeros((depth, d, LANES), BF16).at[:, :, :lowrank].set(w_in[:, :, main:c_col].astype(BF16))
    w_lr2 = jnp.zeros((depth, LANES, gqk), BF16).at[:, :lowrank].set(w_gla_lr2.astype(BF16))
    w_c = w_in[:, :, c_col:]
    cache_k2 = cache_swa_k.reshape(depth, bs * window, akv)
    cache_v2 = cache_swa_v.reshape(depth, bs * window, akv)
    conv_prev_s = jnp.pad(state_conv, ((0, 0), (0, 0), (SUBLANES - (conv_w - 1), 0), (0, 0)))
    conv_prev_p = jnp.zeros((bp, SUBLANES, width), F32)
    wa_bf, wx_bf = w_lru_a.astype(BF16), w_lru_x.astype(BF16)

    n_pos = max(seq, PAST_LEN + dseq)
    cos, sin = _rope_tables(n_pos, hd)

    x = jnp.concatenate([x_prompt.reshape(mp, d), x_sample.reshape(ms, d)], axis=0)
    y = None
    outs = {k: [] for k in ("pk", "pv", "pg", "pc", "pl", "sk", "sv", "sg", "sc", "sl")}
    tm_ffn = FFN_ROW_TILE if m % FFN_ROW_TILE == 0 else _tile(m, FFN_ROW_TILE)
    for l in range(depth):
        x, n = _addnorm(x, y, ln_mix[l], BF16)
        u = _mm(n, w_in, l, 0, main)
        uc = _mm(n, w_c, l, 0, 2 * width)
        lg = _lowrank_gate(n, w_lr[l], w_lr2[l], b_gla_gate[l].reshape(1, gqk))

        swa = functools.partial(_swa, u, attn_sink[l], cos, sin, n_kv=n_kv, groups=groups, hd=hd, aq=aq)
        oa_p, kr_p = swa(None, None, n_seq=bp, n_chunks=nc, row0=0, pos0=0)
        oa_s, kr_s = swa(cache_k2[l], cache_v2[l], n_seq=bs, n_chunks=1, row0=mp, pos0=PAST_LEN)

        gla = functools.partial(_gla, u, lg, g_gla_norm[l], heads=gh, dk=dk, dv=dv, q_col=aq + 2 * akv,
                                k_col=aq + 2 * akv + gqk, v_col=aq + 2 * akv + 2 * gqk,
                                r_col=aq + 2 * akv + 2 * gqk + gv)
        ob_p, sg_p = gla(None, n_seq=bp, n_chunks=nc, row0=0)
        ob_s, sg_s = gla(state_gla[l], n_seq=bs, n_chunks=1, row0=mp)

        lru = functools.partial(_lru, uc, wc=w_conv[l], bc=b_conv[l].reshape(1, width), wa=wa_bf[l],
                                ba=b_lru_a[l].reshape(1, width), wx=wx_bf[l], bx=b_lru_x[l].reshape(1, width),
                                lam=lru_lambda[l].reshape(1, width), width=width)
        oc_p, hl_p = lru(conv_prev_p, jnp.zeros((bp, 1, width), F32), n_seq=bp, n_chunks=nc, row0=0,
                         stream_start=True)
        oc_s, hl_s = lru(conv_prev_s[l], state_lru[l].reshape(bs, 1, width), n_seq=bs, n_chunks=1, row0=mp,
                         stream_start=False)

        merged = _merge(n, (oa_p, ob_p, oc_p), (oa_s, ob_s, oc_s), w_gate, b_gate.reshape(depth, 1, -1), w_branch, l)
        y = _mm(merged, w_out, l, 0, d)
        x, n2 = _addnorm(x, y, ln_ffn[l], BF16)
        if l % 2 == 0:
            i = l // 2
            tiles = m // tm_ffn
            y = _ffn(n2, w_ff_gate[i:i + 1], w_ff_up[i:i + 1], w_ff_down[i:i + 1], jnp.zeros((tiles,), jnp.int32),
                     jnp.full((1,), tiles, jnp.int32), jnp.ones((tiles,), jnp.int32), tm_ffn, tm_ffn)
        else:
            i = l // 2
            x = _moe(x, ln_ffn[l], w_router[i], w_moe_gate[i], w_moe_up[i], w_moe_down[i], tm_ffn)
            y = None

        def tail_rows(arr, b, t, keep_rows, c0, c1):
            return jnp.stack([arr[i * t + t - keep_rows:(i + 1) * t, c0:c1] for i in range(b)])

        v0, v1 = aq + akv, aq + 2 * akv
        keep = window - dseq
        outs["pk"].append(tail_rows(kr_p, bp, seq, window, 0, akv).reshape(bp, window, n_kv, hd))
        outs["pv"].append(tail_rows(u, bp, seq, window, v0, v1).reshape(bp, window, n_kv, hd))
        outs["pg"].append(sg_p)
        outs["pc"].append(tail_rows(uc, bp, seq, conv_w - 1, 0, width))
        outs["pl"].append(hl_p.reshape(bp, width))
        outs["sk"].append(jnp.concatenate([cache_swa_k[l][:, window - keep:], kr_s.reshape(bs, dseq, n_kv, hd)], 1))
        outs["sv"].append(jnp.concatenate([cache_swa_v[l][:, window - keep:],
                                           u[mp:, v0:v1].reshape(bs, dseq, n_kv, hd)], 1))
        outs["sg"].append(sg_s)
        outs["sc"].append(uc[mp:, :width].reshape(bs, dseq, width)[:, dseq - (conv_w - 1):])
        outs["sl"].append(hl_s.reshape(bs, width))

    yp, ys = _final_norm(x, y, ln_final, mp)
    st = {k: jnp.stack(v) for k, v in outs.items()}
    return (yp.reshape(bp, seq, d), ys.reshape(bs, dseq, d),
            st["pk"], st["pv"], st["pg"], st["pc"], st["pl"],
            st["sk"], st["sv"], st["sg"], st["sc"], st["sl"])
```

```python
import functools

import jax
import jax.numpy as jnp
from jax import lax
from jax.experimental import pallas as pl
from jax.experimental.pallas import tpu as pltpu

F32 = jnp.float32
BF16 = jnp.bfloat16

CHUNK = 64
PAST_LEN = 2048
TOP_K = 2
EPS = 1e-6
NEG_INF = -1e30
ROPE_THETA = 10000.0
GLA_TAU = 16.0
LRU_C = 8.0

LANES = 128
SUBLANES = 8
V7X_VMEM_BYTES = 64 * 1024 * 1024
MIB = 1024 * 1024
FFN_ROW_TILE = 1024
FFN_DOWN_COLS = 512
FFN_SUB_ROWS = 256
MM_ROW_TILE = 2304


def _params(semantics, vmem_mib):
    assert vmem_mib * MIB < V7X_VMEM_BYTES
    return pltpu.CompilerParams(dimension_semantics=semantics, vmem_limit_bytes=vmem_mib * MIB)


def _tile(n, target):
    t = 1
    while t * 2 <= target and n % (t * 2) == 0:
        t *= 2
    return t


def _rope_table_kernel(cos_ref, sin_ref):
    rows, hd = cos_ref.shape
    half = hd // 2
    lane = lax.broadcasted_iota(jnp.int32, (rows, hd), 1)
    j = jnp.where(lane < half, lane, lane - half).astype(F32)
    inv = jnp.power(jnp.float32(ROPE_THETA), -j / half)
    pos = (lax.broadcasted_iota(jnp.int32, (rows, hd), 0) + pl.program_id(0) * rows).astype(F32)
    ang = pos * inv
    cos_ref[...] = jnp.cos(ang)
    s = jnp.sin(ang)
    sin_ref[...] = jnp.where(lane < half, -s, s)


def _rope_tables(n_pos, head_dim):
    rows = _tile(n_pos, 512)
    return pl.pallas_call(
        _rope_table_kernel,
        grid=(n_pos // rows,),
        out_specs=[pl.BlockSpec((rows, head_dim), lambda i: (i, 0))] * 2,
        out_shape=[jax.ShapeDtypeStruct((n_pos, head_dim), F32)] * 2,
        compiler_params=_params(("arbitrary",), 16),
        name="rope_tables",
    )()


def _rms(xf, g):
    return xf * lax.rsqrt(jnp.mean(xf * xf, axis=-1, keepdims=True) + EPS) * g


def _addnorm_kernel(*refs, has_y, emit_x):
    it = iter(refs)
    x_ref = next(it)
    y_ref = next(it) if has_y else None
    g_ref = next(it)
    xo_ref = next(it) if emit_x else None
    n_ref = next(it)
    x = x_ref[...]
    if has_y:
        x = x + y_ref[...]
    if emit_x:
        xo_ref[...] = x
    n_ref[...] = _rms(x, g_ref[...]).astype(n_ref.dtype)


def _addnorm(x, y, g, out_dtype):
    m, d = x.shape
    tm = _tile(m, 256)
    has_y = y is not None
    row = pl.BlockSpec((tm, d), lambda i: (i, 0))
    ins = [x] + ([y] if has_y else []) + [g.reshape(1, d)]
    in_specs = [row] * (2 if has_y else 1) + [pl.BlockSpec((1, d), lambda i: (0, 0))]
    out_shape = ([jax.ShapeDtypeStruct((m, d), F32)] if has_y else []) + [jax.ShapeDtypeStruct((m, d), out_dtype)]
    outs = pl.pallas_call(
        functools.partial(_addnorm_kernel, has_y=has_y, emit_x=has_y),
        grid=(m // tm,),
        in_specs=in_specs,
        out_specs=[row] * len(out_shape),
        out_shape=out_shape,
        compiler_params=_params(("arbitrary",), 48),
        name="add_rmsnorm",
    )(*ins)
    return (outs[0], outs[1]) if has_y else (x, outs[0])


def _final_norm_kernel(*refs, has_y, prompt_tiles):
    it = iter(refs)
    x_ref = next(it)
    y_ref = next(it) if has_y else None
    g_ref, p_ref, s_ref = next(it), next(it), next(it)
    i = pl.program_id(0)
    x = x_ref[...]
    if has_y:
        x = x + y_ref[...]
    n = _rms(x, g_ref[...])

    @pl.when(i < prompt_tiles)
    def _():
        p_ref[...] = n

    @pl.when(i >= prompt_tiles)
    def _():
        s_ref[...] = n


def _final_norm(x, y, g, mp):
    m, d = x.shape
    ms = m - mp
    tm = min(_tile(mp, 256), _tile(ms, 256))
    pt = mp // tm
    has_y = y is not None
    row = pl.BlockSpec((tm, d), lambda i: (i, 0))
    return pl.pallas_call(
        functools.partial(_final_norm_kernel, has_y=has_y, prompt_tiles=pt),
        grid=(m // tm,),
        in_specs=[row] * (2 if has_y else 1) + [pl.BlockSpec((1, d), lambda i: (0, 0))],
        out_specs=[pl.BlockSpec((tm, d), lambda i: (jnp.minimum(i, pt - 1), 0)),
                   pl.BlockSpec((tm, d), lambda i: (jnp.maximum(i - pt, 0), 0))],
        out_shape=[jax.ShapeDtypeStruct((mp, d), F32), jax.ShapeDtypeStruct((ms, d), F32)],
        compiler_params=_params(("arbitrary",), 48),
        name="final_rmsnorm",
    )(*([x] + ([y] if has_y else []) + [g.reshape(1, d)]))


def _mm_kernel(x_ref, w_ref, o_ref):
    o_ref[...] = jnp.dot(x_ref[...], w_ref[...].astype(BF16), preferred_element_type=F32).astype(o_ref.dtype)


def _mm(x, w, layer, col0, n, out_dtype=F32):
    m, k = x.shape
    tm = MM_ROW_TILE if m % MM_ROW_TILE == 0 else _tile(m, 1024)
    tn = _tile(n, 512)
    assert col0 % tn == 0 and tn % LANES == 0
    off = col0 // tn
    return pl.pallas_call(
        _mm_kernel,
        grid=(m // tm, n // tn),
        in_specs=[
            pl.BlockSpec((tm, k), lambda i, j: (i, 0), pipeline_mode=pl.Buffered(1)),
            pl.BlockSpec((None, k, tn), lambda i, j: (layer, 0, j + off)),
        ],
        out_specs=pl.BlockSpec((tm, tn), lambda i, j: (i, j)),
        out_shape=jax.ShapeDtypeStruct((m, n), out_dtype),
        compiler_params=_params(("arbitrary", "arbitrary"), 56),
        name="matmul",
    )(x, w)


def _log_sigmoid(z):
    y = -z
    return -(jnp.maximum(y, 0.0) + jnp.log1p(jnp.exp(-jnp.abs(y))))


def _lowrank_kernel(n_ref, w1_ref, w2_ref, b_ref, o_ref):
    glr = jnp.dot(n_ref[...], w1_ref[...], preferred_element_type=F32)
    z = jnp.dot(glr.astype(BF16), w2_ref[...], preferred_element_type=F32) + b_ref[...]
    o_ref[...] = _log_sigmoid(z) / GLA_TAU


def _lowrank_gate(n, w1, w2, b):
    m, d = n.shape
    gqk = w2.shape[1]
    tm = _tile(m, 512)
    return pl.pallas_call(
        _lowrank_kernel,
        grid=(m // tm,),
        in_specs=[
            pl.BlockSpec((tm, d), lambda i: (i, 0)),
            pl.BlockSpec(w1.shape, lambda i: (0, 0)),
            pl.BlockSpec(w2.shape, lambda i: (0, 0)),
            pl.BlockSpec((1, gqk), lambda i: (0, 0)),
        ],
        out_specs=pl.BlockSpec((tm, gqk), lambda i: (i, 0)),
        out_shape=jax.ShapeDtypeStruct((m, gqk), F32),
        compiler_params=_params(("arbitrary",), 32),
        name="gla_forget_gate",
    )(n, w1, w2, b)


def _rope(x, cos, sin):
    return x * cos + pltpu.roll(x, x.shape[-1] // 2, 1) * sin


def _swa_kernel(sink_ref, q_ref, k0_ref, k1_ref, k2_ref, v0_ref, v1_ref, v2_ref,
                c0_ref, c1_ref, c2_ref, s0_ref, s1_ref, s2_ref, o_ref, krot_ref, *, groups, cached):
    c = pl.program_id(1)
    hd = c2_ref.shape[-1]
    n_kv = k2_ref.shape[-1] // hd
    cosq, sinq = c2_ref[...], s2_ref[...]
    for kvh in range(n_kv):
        ks = slice(kvh * hd, (kvh + 1) * hd)
        k2 = _rope(k2_ref[:, ks], cosq, sinq)
        krot_ref[:, ks] = k2
        if cached:
            k0, k1 = k0_ref[:, ks], k1_ref[:, ks]
        else:
            k0 = _rope(k0_ref[:, ks], c0_ref[...], s0_ref[...])
            k1 = _rope(k1_ref[:, ks], c1_ref[...], s1_ref[...])
        kall = jnp.concatenate([k0, k1, k2], axis=0).astype(BF16)
        vall = jnp.concatenate([v0_ref[:, ks], v1_ref[:, ks], v2_ref[:, ks]], axis=0).astype(BF16)
        q0 = kvh * groups * hd
        qall = jnp.concatenate(
            [_rope(q_ref[:, q0 + g * hd:q0 + (g + 1) * hd], cosq, sinq) for g in range(groups)], axis=0).astype(BF16)
        s = lax.dot_general(qall, kall, (((1,), (1,)), ((), ())), preferred_element_type=F32) * hd ** -0.5
        if not cached:
            col = lax.broadcasted_iota(jnp.int32, s.shape, 1)
            s = jnp.where(col >= jnp.maximum(2 - c, 0) * CHUNK, s, NEG_INF)
        sink = jnp.concatenate(
            [jnp.full((CHUNK, 1), sink_ref[kvh * groups + g], F32) for g in range(groups)], axis=0)
        mx = jnp.maximum(jnp.max(s, axis=-1, keepdims=True), sink)
        p = jnp.exp(s - mx)
        den = jnp.sum(p, axis=-1, keepdims=True) + jnp.exp(sink - mx)
        p = p / den
        o = jnp.dot(p.astype(BF16), vall, preferred_element_type=F32)
        for g in range(groups):
            o_ref[:, q0 + g * hd:q0 + (g + 1) * hd] = o[g * CHUNK:(g + 1) * CHUNK].astype(o_ref.dtype)


def _swa(u, sink, cos, sin, cache_k, cache_v, *, n_seq, n_chunks, row0, n_kv, groups, hd, pos0, aq):
    cached = cache_k is not None
    akv = n_kv * hd
    assert aq % akv == 0
    rb0 = row0 // CHUNK
    pb0 = pos0 // CHUNK
    kcol, vcol = aq // akv, aq // akv + 1

    def rowblk(b, c):
        return rb0 + b * n_chunks + c

    def window(colblk):
        return [pl.BlockSpec((CHUNK, akv), lambda b, c, j=j: (rowblk(b, jnp.maximum(c - 2 + j, 0)), colblk))
                for j in range(2)]

    if cached:
        kprev = vprev = [pl.BlockSpec((CHUNK, akv), lambda b, c, j=j: (2 * b + j, 0)) for j in range(2)]
        kin, vin = [cache_k, cache_k], [cache_v, cache_v]
    else:
        kprev, vprev = window(kcol), window(vcol)
        kin = vin = [u, u]
    q_spec = pl.BlockSpec((CHUNK, aq), lambda b, c: (rowblk(b, c), 0))
    kcur = pl.BlockSpec((CHUNK, akv), lambda b, c: (rowblk(b, c), kcol))
    vcur = pl.BlockSpec((CHUNK, akv), lambda b, c: (rowblk(b, c), vcol))
    tab = [pl.BlockSpec((CHUNK, hd), lambda b, c, j=j: (pb0 + jnp.maximum(c - 2 + j, 0), 0)) for j in range(3)]
    out_rows = n_seq * n_chunks * CHUNK
    return pl.pallas_call(
        functools.partial(_swa_kernel, groups=groups, cached=cached),
        grid=(n_seq, n_chunks),
        in_specs=[pl.BlockSpec(memory_space=pltpu.SMEM), q_spec] + kprev + [kcur] + vprev + [vcur] + tab + tab,
        out_specs=[pl.BlockSpec((CHUNK, aq), lambda b, c: (b * n_chunks + c, 0)),
                   pl.BlockSpec((CHUNK, akv), lambda b, c: (b * n_chunks + c, 0))],
        out_shape=[jax.ShapeDtypeStruct((out_rows, aq), BF16), jax.ShapeDtypeStruct((out_rows, akv), F32)],
        compiler_params=_params(("arbitrary",) * 2, 32),
        name="swa_cached" if cached else "swa_prompt",
    )(sink, u, *kin, u, *vin, u, cos, cos, cos, sin, sin, sin)


def _cumsum_rows(x):
    row = lax.broadcasted_iota(jnp.int32, x.shape, 0)
    d = 1
    while d < x.shape[0]:
        x = x + jnp.where(row >= d, pltpu.roll(x, d, 0), 0.0)
        d *= 2
    return x


def _gla_kernel(*refs, heads, has_init):
    it = iter(refs)
    q_ref, k_ref, lg_ref, g_ref = (next(it) for _ in range(4))
    v_refs = [next(it) for _ in range(heads)]
    r_refs = [next(it) for _ in range(heads)]
    s0_ref = next(it) if has_init else None
    o_ref, sout_ref, st_ref = next(it), next(it), next(it)
    c = pl.program_id(1)
    dk = q_ref.shape[-1] // heads
    dv = v_refs[0].shape[-1]

    @pl.when(c == 0)
    def _():
        for h in range(heads):
            st_ref[h] = s0_ref[h].T if has_init else jnp.zeros((dv, dk), F32)

    tri = lax.broadcasted_iota(jnp.int32, (CHUNK, CHUNK), 0) >= lax.broadcasted_iota(jnp.int32, (CHUNK, CHUNK), 1)
    for h in range(heads):
        ks = slice(h * dk, (h + 1) * dk)
        b = _cumsum_rows(lg_ref[:, ks])
        b_last = b[CHUNK - 1:CHUNK, :]
        k = k_ref[:, ks]
        qe = (q_ref[:, ks] * dk ** -0.5 * jnp.exp(b)).astype(BF16)
        ke = (k * jnp.exp(-b)).astype(BF16)
        kd = (k * jnp.exp(b_last - b)).astype(BF16)
        v = v_refs[h][...].astype(BF16)
        att = lax.dot_general(qe, ke, (((1,), (1,)), ((), ())), preferred_element_type=F32)
        att = jnp.where(tri, att, 0.0)
        st = st_ref[h]
        o = lax.dot_general(qe, st.astype(BF16), (((1,), (1,)), ((), ())), preferred_element_type=F32)
        o = o + jnp.dot(att.astype(BF16), v, preferred_element_type=F32)
        st_new = jnp.exp(b_last) * st + lax.dot_general(v, kd, (((0,), (0,)), ((), ())), preferred_element_type=F32)
        st_ref[h] = st_new
        r = r_refs[h][...]
        o_ref[:, h * dv:(h + 1) * dv] = (_rms(o, g_ref[...]) * (r * jax.nn.sigmoid(r))).astype(o_ref.dtype)

    @pl.when(c == pl.num_programs(1) - 1)
    def _():
        for h in range(heads):
            sout_ref[h] = st_ref[h].T


def _gla(u, lg, g_norm, s0, *, n_seq, n_chunks, row0, heads, dk, dv, q_col, k_col, v_col, r_col):
    has_init = s0 is not None
    rb0 = row0 // CHUNK
    gqk = heads * dk
    assert q_col % gqk == 0 and k_col % gqk == 0 and v_col % dv == 0 and r_col % dv == 0

    def col(width, blk):
        return pl.BlockSpec((CHUNK, width), lambda b, c: (rb0 + b * n_chunks + c, blk))

    in_specs = ([col(gqk, q_col // gqk), col(gqk, k_col // gqk), col(gqk, 0), pl.BlockSpec((1, dv), lambda b, c: (0, 0))]
                + [col(dv, v_col // dv + h) for h in range(heads)] + [col(dv, r_col // dv + h) for h in range(heads)])
    ins = [u, u, lg, g_norm.reshape(1, dv)] + [u] * (2 * heads)
    if has_init:
        in_specs.append(pl.BlockSpec((None, heads, dk, dv), lambda b, c: (b, 0, 0, 0)))
        ins.append(s0)
    out_rows = n_seq * n_chunks * CHUNK
    return pl.pallas_call(
        functools.partial(_gla_kernel, heads=heads, has_init=has_init),
        grid=(n_seq, n_chunks),
        in_specs=in_specs,
        out_specs=[pl.BlockSpec((CHUNK, heads * dv), lambda b, c: (b * n_chunks + c, 0)),
                   pl.BlockSpec((None, heads, dk, dv), lambda b, c: (b, 0, 0, 0))],
        out_shape=[jax.ShapeDtypeStruct((out_rows, heads * dv), BF16),
                   jax.ShapeDtypeStruct((n_seq, heads, dk, dv), F32)],
        scratch_shapes=[pltpu.VMEM((heads, dv, dk), F32)],
        compiler_params=_params(("arbitrary",) * 2, 40),
        name="gla_init" if has_init else "gla_prompt",
    )(*ins)


def _softplus(y):
    return jnp.maximum(y, 0.0) + jnp.log1p(jnp.exp(-jnp.abs(y)))


def _lru_kernel(x_ref, xprev_ref, cprev_ref, y_ref, wc_ref, bc_ref, wa_ref, ba_ref, wx_ref, bx_ref, lam_ref, h0_ref,
                o_ref, hl_ref, h_ref, a_ref, u_ref, *, stream_start, conv_w):
    c = pl.program_id(1)
    t, w = x_ref.shape
    nblk, bw = wa_ref.shape[0], wa_ref.shape[1]

    @pl.when(c == 0)
    def _():
        h_ref[...] = h0_ref[...]

    prev = jnp.where(c == 0, cprev_ref[...], xprev_ref[...])
    ext = jnp.concatenate([prev, x_ref[...]], axis=0)
    xc = bc_ref[...]
    for j in range(conv_w):
        shift = conv_w - 1 - j
        xs = ext if shift == 0 else pltpu.roll(ext, shift, 0)
        xc = xc + xs[SUBLANES:] * wc_ref[j:j + 1, :]
    xb = xc.astype(BF16)
    ra = jnp.concatenate([jnp.dot(xb[:, n * bw:(n + 1) * bw], wa_ref[n], preferred_element_type=F32)
                          for n in range(nblk)], axis=1)
    ia = jnp.concatenate([jnp.dot(xb[:, n * bw:(n + 1) * bw], wx_ref[n], preferred_element_type=F32)
                          for n in range(nblk)], axis=1)
    r = jax.nn.sigmoid(ra + ba_ref[...])
    gate = jax.nn.sigmoid(ia + bx_ref[...])
    log_a = -LRU_C * r * _softplus(-lam_ref[...])
    a = jnp.exp(log_a)
    mult = jnp.sqrt(jnp.tanh(-log_a) * (a * a + 1.0))
    if stream_start:
        first = (lax.broadcasted_iota(jnp.int32, (t, w), 0) == 0) & (c == 0)
        mult = jnp.where(first, 1.0, mult)
        a = jnp.where(first, 0.0, a)
    a_ref[...] = a
    u_ref[...] = mult * gate * xc

    def step(i, h):
        h = a_ref[pl.ds(i, 1), :] * h + u_ref[pl.ds(i, 1), :]
        u_ref[pl.ds(i, 1), :] = h
        return h

    h = lax.fori_loop(0, t, step, h_ref[...], unroll=8)
    h_ref[...] = h
    hl_ref[...] = h
    o_ref[...] = (jax.nn.gelu(y_ref[...]) * u_ref[...]).astype(o_ref.dtype)


def _lru(uc, cprev8, h0, wc, bc, wa, ba, wx, bx, lam, *, n_seq, n_chunks, row0, width, stream_start):
    rb0 = row0 // CHUNK
    per = CHUNK // SUBLANES
    conv_w = wc.shape[0]
    vec = pl.BlockSpec((1, width), lambda b, c: (0, 0))
    blk = pl.BlockSpec(wa.shape, lambda b, c: (0, 0, 0))
    out_rows = n_seq * n_chunks * CHUNK
    return pl.pallas_call(
        functools.partial(_lru_kernel, stream_start=stream_start, conv_w=conv_w),
        grid=(n_seq, n_chunks),
        in_specs=[
            pl.BlockSpec((CHUNK, width), lambda b, c: (rb0 + b * n_chunks + c, 0)),
            pl.BlockSpec((SUBLANES, width), lambda b, c: (jnp.maximum((rb0 + b * n_chunks + c) * per - 1, 0), 0)),
            pl.BlockSpec((None, SUBLANES, width), lambda b, c: (b, 0, 0)),
            pl.BlockSpec((CHUNK, width), lambda b, c: (rb0 + b * n_chunks + c, 1)),
            pl.BlockSpec(wc.shape, lambda b, c: (0, 0)), vec, blk, vec, blk, vec, vec,
            pl.BlockSpec((None, 1, width), lambda b, c: (b, 0, 0)),
        ],
        out_specs=[pl.BlockSpec((CHUNK, width), lambda b, c: (b * n_chunks + c, 0)),
                   pl.BlockSpec((None, 1, width), lambda b, c: (b, 0, 0))],
        out_shape=[jax.ShapeDtypeStruct((out_rows, width), BF16), jax.ShapeDtypeStruct((n_seq, 1, width), F32)],
        scratch_shapes=[pltpu.VMEM((1, width), F32), pltpu.VMEM((CHUNK, width), F32), pltpu.VMEM((CHUNK, width), F32)],
        compiler_params=_params(("arbitrary",) * 2, 32),
        name="conv_rglru",
    )(uc, uc, cprev8, uc, wc, bc, wa, ba, wx, bx, lam, h0)


def _merge_kernel(n_ref, ap_ref, bp_ref, cp_ref, as_ref, bs_ref, cs_ref, wg0_ref, wg1_ref, wg2_ref, wb_ref,
                  bg0_ref, bg1_ref, bg2_ref, o_ref, wgbf_ref, wbbf_ref, *, prompt_tiles):
    i = pl.program_id(1)

    @pl.when(i == 0)
    def _():
        for b, wg_ref in enumerate((wg0_ref, wg1_ref, wg2_ref)):
            wgbf_ref[b] = wg_ref[...].astype(BF16)
            wbbf_ref[b] = wb_ref[b].astype(BF16)

    n = n_ref[...]
    acc = None
    branches = ((ap_ref, as_ref, bg0_ref), (bp_ref, bs_ref, bg1_ref), (cp_ref, cs_ref, bg2_ref))
    for b, (p_ref, s_ref, bg_ref) in enumerate(branches):
        br = jnp.where(i < prompt_tiles, p_ref[...], s_ref[...])
        gate = jax.nn.sigmoid(jnp.dot(n, wgbf_ref[b], preferred_element_type=F32) + bg_ref[...])
        term = gate * jnp.dot(br, wbbf_ref[b], preferred_element_type=F32)
        acc = term if acc is None else acc + term
    o_ref[...] = acc.astype(o_ref.dtype)


def _merge(n, prompt, sample, w_gate, b_gate, w_branch, layer):
    m, d = n.shape
    mp, bw = prompt[0].shape
    ms = sample[0].shape[0]
    nb = w_branch.shape[1]
    assert nb == 3 and mp + ms == m
    tm = min(_tile(mp, 512), _tile(ms, 512))
    tn = _tile(d, 256)
    nj = d // tn
    pt = mp // tm
    single = pl.Buffered(1)
    wg_specs = [pl.BlockSpec((None, d, tn), lambda j, i, b=b: (layer, 0, b * nj + j), pipeline_mode=single)
                for b in range(nb)]
    bg_specs = [pl.BlockSpec((None, 1, tn), lambda j, i, b=b: (layer, 0, b * nj + j)) for b in range(nb)]
    p_spec = pl.BlockSpec((tm, bw), lambda j, i: (jnp.minimum(i, pt - 1), 0))
    s_spec = pl.BlockSpec((tm, bw), lambda j, i: (jnp.maximum(i - pt, 0), 0), pipeline_mode=single)
    return pl.pallas_call(
        functools.partial(_merge_kernel, prompt_tiles=pt),
        grid=(nj, m // tm),
        in_specs=[pl.BlockSpec((tm, d), lambda j, i: (i, 0))] + [p_spec] * nb + [s_spec] * nb + wg_specs
        + [pl.BlockSpec((None, nb, bw, tn), lambda j, i: (layer, 0, 0, j), pipeline_mode=single)] + bg_specs,
        out_specs=pl.BlockSpec((tm, tn), lambda j, i: (i, j)),
        out_shape=jax.ShapeDtypeStruct((m, d), BF16),
        scratch_shapes=[pltpu.VMEM((nb, d, tn), BF16), pltpu.VMEM((nb, bw, tn), BF16)],
        compiler_params=_params(("arbitrary", "arbitrary"), 60),
        name="gated_merge",
    )(n, *prompt, *sample, w_gate, w_gate, w_gate, w_branch, b_gate, b_gate, b_gate)


def _ffn_kernel(te_ref, nv_ref, ns_ref, x_ref, wg_ref, wu_ref, wd_ref, o_ref, *bf_refs, sub):
    i, j = pl.program_id(0), pl.program_id(1)
    tm, d = o_ref.shape
    dc = min(d, FFN_DOWN_COLS)
    nsub = tm // sub
    used = ns_ref[i]

    def rows_body(rows, wg, wu, wd_cols):
        x = x_ref[rows, :]
        g = jnp.dot(x, wg, preferred_element_type=F32)
        u = jnp.dot(x, wu, preferred_element_type=F32)
        h = (g * jax.nn.sigmoid(g) * u).astype(BF16)

        @pl.when(j == 0)
        def _():
            for n0 in range(0, d, dc):
                o_ref[rows, n0:n0 + dc] = jnp.dot(h, wd_cols(n0), preferred_element_type=F32)

        @pl.when(j > 0)
        def _():
            for n0 in range(0, d, dc):
                o_ref[rows, n0:n0 + dc] += jnp.dot(h, wd_cols(n0), preferred_element_type=F32)

    @pl.when(used == nsub)
    def _():
        rows_body(slice(0, tm), wg_ref[...].astype(BF16), wu_ref[...].astype(BF16),
                  lambda n0: wd_ref[:, n0:n0 + dc].astype(BF16))

    if nsub > 1:
        wgbf_ref, wubf_ref, wdbf_ref = bf_refs

        @pl.when((used > 0) & (used < nsub))
        def _():
            wg, wu, wd = wg_ref[...].astype(BF16), wu_ref[...].astype(BF16), wd_ref[...].astype(BF16)
            wgbf_ref[...] = wg
            wubf_ref[...] = wu
            wdbf_ref[...] = wd
            rows_body(slice(0, sub), wg, wu, lambda n0: wd[:, n0:n0 + dc])
            for s in range(1, nsub - 1):
                @pl.when(s < used)
                def _(s=s):
                    rows_body(slice(s * sub, (s + 1) * sub), wgbf_ref[...], wubf_ref[...],
                              lambda n0: wdbf_ref[:, n0:n0 + dc])

    for s in range(nsub):
        @pl.when((s >= used) & (j == 0))
        def _(s=s):
            o_ref[s * sub:(s + 1) * sub, :] = jnp.zeros((sub, d), o_ref.dtype)


def _ffn(x, wg, wu, wd, tile_expert, n_valid, n_sub, tm, sub):
    r, d = x.shape
    f = wg.shape[2]
    tf = _tile(f, 256)
    nj = f // tf
    single = pl.Buffered(1)

    def last(i, nv):
        return jnp.maximum(jnp.minimum(i, nv[0] - 1), 0)

    def row(i, j, te, nv, ns):
        return (last(i, nv), 0)

    def fidx(i, j, nv):
        return jnp.where(i < nv[0], j, nj - 1)

    def up(i, j, te, nv, ns):
        return (te[last(i, nv)], 0, fidx(i, j, nv))

    def down(i, j, te, nv, ns):
        return (te[last(i, nv)], fidx(i, j, nv), 0)

    return pl.pallas_call(
        functools.partial(_ffn_kernel, sub=sub),
        grid_spec=pltpu.PrefetchScalarGridSpec(
            num_scalar_prefetch=3,
            grid=(r // tm, nj),
            in_specs=[pl.BlockSpec((tm, d), row, pipeline_mode=single),
                      pl.BlockSpec((None, d, tf), up), pl.BlockSpec((None, d, tf), up),
                      pl.BlockSpec((None, tf, d), down)],
            out_specs=pl.BlockSpec((tm, d), lambda i, j, te, nv, ns: (i, 0), pipeline_mode=single),
            scratch_shapes=([pltpu.VMEM((d, tf), BF16), pltpu.VMEM((d, tf), BF16), pltpu.VMEM((tf, d), BF16)]
                            if sub < tm else []),
        ),
        out_shape=jax.ShapeDtypeStruct((r, d), F32),
        compiler_params=_params(("arbitrary", "arbitrary"), 60),
        name="swiglu_ffn",
    )(tile_expert, n_valid, n_sub, x, wg, wu, wd)


def _router_kernel(x_ref, g_ref, w_ref, meta_ref, cnt_ref, carry_ref, *, n_experts):
    @pl.when(pl.program_id(0) == 0)
    def _():
        carry_ref[...] = jnp.zeros_like(carry_ref)

    tm = x_ref.shape[0]
    n = _rms(x_ref[...], g_ref[...])
    logits = jnp.dot(n, w_ref[...], preferred_element_type=F32, precision=lax.Precision.HIGHEST)
    lane = lax.broadcasted_iota(jnp.int32, logits.shape, 1).astype(F32)
    logits = jnp.where(lane < n_experts, logits, -jnp.inf)
    m1 = jnp.max(logits, axis=-1, keepdims=True)
    i1 = jnp.min(jnp.where(logits == m1, lane, float(LANES)), axis=-1, keepdims=True)
    rest = jnp.where(lane == i1, -jnp.inf, logits)
    m2 = jnp.max(rest, axis=-1, keepdims=True)
    i2 = jnp.min(jnp.where(rest == m2, lane, float(LANES)), axis=-1, keepdims=True)
    e2 = jnp.exp(m2 - m1)
    den = 1.0 + e2
    oh1 = lane == i1
    oh2 = lane == i2
    both = jnp.where(oh1 | oh2, 1.0, 0.0)
    rr = lax.broadcasted_iota(jnp.int32, (tm, tm), 0)
    cc = lax.broadcasted_iota(jnp.int32, (tm, tm), 1)
    below = jnp.where(rr > cc, 1.0, 0.0).astype(BF16)
    before = jnp.dot(below, both.astype(BF16), preferred_element_type=F32) + carry_ref[...]
    rank1 = jnp.sum(jnp.where(oh1, before, 0.0), axis=-1, keepdims=True)
    rank2 = jnp.sum(jnp.where(oh2, before, 0.0), axis=-1, keepdims=True)
    cols = (i1, i2, 1.0 / den, e2 / den, rank1, rank2)
    meta = jnp.zeros(logits.shape, F32)
    for idx, val in enumerate(cols):
        meta = jnp.where(lane == idx, val, meta)
    meta_ref[...] = meta
    total = carry_ref[...] + jnp.sum(both, axis=0, keepdims=True)
    carry_ref[...] = total
    cnt_ref[...] = total


def _router(x, g, w_pad, n_experts):
    m, d = x.shape
    tm = _tile(m, 256)
    return pl.pallas_call(
        functools.partial(_router_kernel, n_experts=n_experts),
        grid=(m // tm,),
        in_specs=[pl.BlockSpec((tm, d), lambda i: (i, 0)), pl.BlockSpec((1, d), lambda i: (0, 0)),
                  pl.BlockSpec(w_pad.shape, lambda i: (0, 0))],
        out_specs=[pl.BlockSpec((tm, LANES), lambda i: (i, 0)), pl.BlockSpec((1, LANES), lambda i: (0, 0))],
        out_shape=[jax.ShapeDtypeStruct((m, LANES), F32), jax.ShapeDtypeStruct((1, LANES), F32)],
        scratch_shapes=[pltpu.VMEM((1, LANES), F32)],
        compiler_params=_params(("arbitrary",), 40),
        name="moe_router",
    )(x, g.reshape(1, d), w_pad)


def _dispatch_kernel(used_ref, src_ref, nxt_ref, x_hbm, g_ref, o_ref, buf_ref, sem):
    i = pl.program_id(0)
    rows = o_ref.shape[0]
    slot = i % 2

    def copy(tbl_ref, r, s):
        return pltpu.make_async_copy(x_hbm.at[pl.ds(tbl_ref[0, r], 1)], buf_ref.at[s, pl.ds(r, 1)], sem.at[s])

    def start_all(tbl_ref, s):
        def body(r, _):
            copy(tbl_ref, r, s).start()
            return 0
        lax.fori_loop(0, rows, body, 0)

    def in_use(step):
        return step * rows < used_ref[0]

    @pl.when((i == 0) & in_use(0))
    def _():
        start_all(src_ref, 0)

    @pl.when((i + 1 < pl.num_programs(0)) & in_use(i + 1))
    def _():
        start_all(nxt_ref, 1 - slot)

    @pl.when(in_use(i))
    def _():
        def body(r, _):
            copy(src_ref, r, slot).wait()
            return 0
        lax.fori_loop(0, rows, body, 0)
        o_ref[...] = _rms(buf_ref[slot], g_ref[...]).astype(o_ref.dtype)

    @pl.when(jnp.logical_not(in_use(i)))
    def _():
        o_ref[...] = jnp.zeros_like(o_ref)


def _dispatch(x, g, src, n_used):
    m, d = x.shape
    n_rows = src.shape[0]
    rows = _tile(n_rows, 256)
    steps = n_rows // rows
    src3 = src.reshape(steps, 1, rows)
    return pl.pallas_call(
        _dispatch_kernel,
        grid=(steps,),
        in_specs=[pl.BlockSpec(memory_space=pltpu.SMEM),
                  pl.BlockSpec((None, 1, rows), lambda i: (i, 0, 0), memory_space=pltpu.SMEM),
                  pl.BlockSpec((None, 1, rows), lambda i: (jnp.minimum(i + 1, steps - 1), 0, 0),
                               memory_space=pltpu.SMEM),
                  pl.BlockSpec(memory_space=pl.ANY), pl.BlockSpec((1, d), lambda i: (0, 0))],
        out_specs=pl.BlockSpec((rows, d), lambda i: (i, 0)),
        out_shape=jax.ShapeDtypeStruct((n_rows, d), BF16),
        scratch_shapes=[pltpu.VMEM((2, rows, d), F32), pltpu.SemaphoreType.DMA((2,))],
        compiler_params=_params(("arbitrary",), 32),
        name="moe_dispatch",
    )(n_used, src3, src3, x, g.reshape(1, d))


def _combine_kernel(dest_ref, x_ref, meta_ref, y_hbm, o_ref, ya_ref, yb_ref, sem, *, rows):
    bufs = (ya_ref, yb_ref)

    def copy(r, k):
        return pltpu.make_async_copy(y_hbm.at[pl.ds(dest_ref[0, TOP_K * r + k], 1)], bufs[k].at[pl.ds(r, 1)], sem)

    def start(r, _):
        for k in range(TOP_K):
            copy(r, k).start()
        return 0

    def wait(r, _):
        for k in range(TOP_K):
            copy(r, k).wait()
        return 0

    lax.fori_loop(0, rows, start, 0)
    lax.fori_loop(0, rows, wait, 0)
    meta = meta_ref[...]
    o_ref[...] = x_ref[...] + (meta[:, 2:3] * ya_ref[...] + meta[:, 3:4] * yb_ref[...])


def _combine(x, meta, y, dest):
    m, d = x.shape
    rows = _tile(m, 128)
    dest3 = dest.reshape(m // rows, 1, rows * TOP_K)
    return pl.pallas_call(
        functools.partial(_combine_kernel, rows=rows),
        grid=(m // rows,),
        in_specs=[pl.BlockSpec((None, 1, rows * TOP_K), lambda i: (i, 0, 0), memory_space=pltpu.SMEM),
                  pl.BlockSpec((rows, d), lambda i: (i, 0)), pl.BlockSpec((rows, LANES), lambda i: (i, 0)),
                  pl.BlockSpec(memory_space=pl.ANY)],
        out_specs=pl.BlockSpec((rows, d), lambda i: (i, 0)),
        out_shape=jax.ShapeDtypeStruct((m, d), F32),
        scratch_shapes=[pltpu.VMEM((rows, d), F32), pltpu.VMEM((rows, d), F32), pltpu.SemaphoreType.DMA(())],
        compiler_params=_params(("arbitrary",), 32),
        name="moe_combine",
    )(dest3, x, meta, y)


def _moe(x, g, w_router, wg, wu, wd, tm):
    m, d = x.shape
    n_exp = w_router.shape[1]
    w_pad = jnp.zeros((d, LANES), F32).at[:, :n_exp].set(w_router)
    meta, counts = _router(x, g, w_pad, n_exp)
    expert = meta[:, 0:TOP_K].astype(jnp.int32)
    rank = meta[:, 4:4 + TOP_K].astype(jnp.int32)
    cnt = counts[0, :n_exp].astype(jnp.int32)
    padded = (cnt + tm - 1) // tm * tm
    ends = jnp.cumsum(padded)
    dest = (ends - padded)[expert] + rank
    n_tiles = (m * TOP_K) // tm + n_exp
    tile_ids = jnp.arange(n_tiles, dtype=jnp.int32)
    tile_expert = jnp.minimum(jnp.sum((ends // tm)[None, :] <= tile_ids[:, None], axis=1), n_exp - 1).astype(jnp.int32)
    n_valid = (ends[-1:] // tm).astype(jnp.int32)
    sub = FFN_SUB_ROWS if tm % FFN_SUB_ROWS == 0 else tm
    used = jnp.clip(cnt[tile_expert] - (tile_ids - ((ends - padded) // tm)[tile_expert]) * tm, 0, tm)
    n_sub = jnp.where(tile_ids < n_valid[0], (used + sub - 1) // sub, 0).astype(jnp.int32)
    token = jnp.broadcast_to(jnp.arange(m, dtype=jnp.int32)[:, None], (m, TOP_K))
    src = jnp.zeros((n_tiles * tm,), jnp.int32).at[dest.reshape(-1)].set(token.reshape(-1))
    ns = _dispatch(x, g, src, ends[-1:].astype(jnp.int32))
    ys = _ffn(ns, wg, wu, wd, tile_expert, n_valid, n_sub, tm, sub)
    return _combine(x, meta, ys, dest)


def kernel(x_prompt, x_sample, cache_swa_k, cache_swa_v, state_gla, state_conv, state_lru, ln_mix, w_in, attn_sink, w_gla_lr2, b_gla_gate, g_gla_norm, w_conv, b_conv, w_lru_a, b_lru_a, w_lru_x, b_lru_x, lru_lambda, w_branch, w_gate, b_gate, w_out, ln_ffn, w_ff_gate, w_ff_up, w_ff_down, w_router, w_moe_gate, w_moe_up, w_moe_down, ln_final):
    bp, seq, d = x_prompt.shape
    bs, dseq, _ = x_sample.shape
    depth = ln_mix.shape[0]
    _, _, window, n_kv, hd = cache_swa_k.shape
    n_heads = attn_sink.shape[1]
    groups = n_heads // n_kv
    _, _, gh, dk, dv = state_gla.shape
    lowrank = w_gla_lr2.shape[1]
    conv_w = w_conv.shape[1]
    width = state_lru.shape[2]
    aq, akv, gqk, gv = n_heads * hd, n_kv * hd, gh * dk, gh * dv
    assert dseq == CHUNK and seq % CHUNK == 0 and window == 2 * CHUNK and hd == LANES and lowrank <= LANES
    mp, ms = bp * seq, bs * dseq
    m = mp + ms
    nc = seq // CHUNK
    main = aq + 2 * akv + 2 * gqk + 2 * gv
    c_col = main + lowrank
    assert w_in.shape[2] == c_col + 2 * width

    w_lr = jnp.zeros((depth, d, LANES), BF16).at[:, :, :lowrank].set(w_in[:, :, main:c_col].astype(BF16))
    w_lr2 = jnp.zeros((depth, LANES, gqk), BF16).at[:, :lowrank].set(w_gla_lr2.astype(BF16))
    w_c = w_in[:, :, c_col:]
    cache_k2 = cache_swa_k.reshape(depth, bs * window, akv)
    cache_v2 = cache_swa_v.reshape(depth, bs * window, akv)
    conv_prev_s = jnp.pad(state_conv, ((0, 0), (0, 0), (SUBLANES - (conv_w - 1), 0), (0, 0)))
    conv_prev_p = jnp.zeros((bp, SUBLANES, width), F32)
    wa_bf, wx_bf = w_lru_a.astype(BF16), w_lru_x.astype(BF16)

    n_pos = max(seq, PAST_LEN + dseq)
    cos, sin = _rope_tables(n_pos, hd)

    x = jnp.concatenate([x_prompt.reshape(mp, d), x_sample.reshape(ms, d)], axis=0)
    y = None
    outs = {k: [] for k in ("pk", "pv", "pg", "pc", "pl", "sk", "sv", "sg", "sc", "sl")}
    tm_ffn = FFN_ROW_TILE if m % FFN_ROW_TILE == 0 else _tile(m, FFN_ROW_TILE)
    for l in range(depth):
        x, n = _addnorm(x, y, ln_mix[l], BF16)
        u = _mm(n, w_in, l, 0, main)
        uc = _mm(n, w_c, l, 0, 2 * width)
        lg = _lowrank_gate(n, w_lr[l], w_lr2[l], b_gla_gate[l].reshape(1, gqk))

        swa = functools.partial(_swa, u, attn_sink[l], cos, sin, n_kv=n_kv, groups=groups, hd=hd, aq=aq)
        oa_p, kr_p = swa(None, None, n_seq=bp, n_chunks=nc, row0=0, pos0=0)
        oa_s, kr_s = swa(cache_k2[l], cache_v2[l], n_seq=bs, n_chunks=1, row0=mp, pos0=PAST_LEN)

        gla = functools.partial(_gla, u, lg, g_gla_norm[l], heads=gh, dk=dk, dv=dv, q_col=aq + 2 * akv,
                                k_col=aq + 2 * akv + gqk, v_col=aq + 2 * akv + 2 * gqk,
                                r_col=aq + 2 * akv + 2 * gqk + gv)
        ob_p, sg_p = gla(None, n_seq=bp, n_chunks=nc, row0=0)
        ob_s, sg_s = gla(state_gla[l], n_seq=bs, n_chunks=1, row0=mp)

        lru = functools.partial(_lru, uc, wc=w_conv[l], bc=b_conv[l].reshape(1, width), wa=wa_bf[l],
                                ba=b_lru_a[l].reshape(1, width), wx=wx_bf[l], bx=b_lru_x[l].reshape(1, width),
                                lam=lru_lambda[l].reshape(1, width), width=width)
        oc_p, hl_p = lru(conv_prev_p, jnp.zeros((bp, 1, width), F32), n_seq=bp, n_chunks=nc, row0=0,
                         stream_start=True)
        oc_s, hl_s = lru(conv_prev_s[l], state_lru[l].reshape(bs, 1, width), n_seq=bs, n_chunks=1, row0=mp,
                         stream_start=False)

        merged = _merge(n, (oa_p, ob_p, oc_p), (oa_s, ob_s, oc_s), w_gate, b_gate.reshape(depth, 1, -1), w_branch, l)
        y = _mm(merged, w_out, l, 0, d)
        x, n2 = _addnorm(x, y, ln_ffn[l], BF16)
        if l % 2 == 0:
            i = l // 2
            tiles = m // tm_ffn
            y = _ffn(n2, w_ff_gate[i:i + 1], w_ff_up[i:i + 1], w_ff_down[i:i + 1], jnp.zeros((tiles,), jnp.int32),
                     jnp.full((1,), tiles, jnp.int32), jnp.ones((tiles,), jnp.int32), tm_ffn, tm_ffn)
        else:
            i = l // 2
            x = _moe(x, ln_ffn[l], w_router[i], w_moe_gate[i], w_moe_up[i], w_moe_down[i], tm_ffn)
            y = None

        def tail_rows(arr, b, t, keep_rows, c0, c1):
            return jnp.stack([arr[i * t + t - keep_rows:(i + 1) * t, c0:c1] for i in range(b)])

        v0, v1 = aq + akv, aq + 2 * akv
        keep = window - dseq
        outs["pk"].append(tail_rows(kr_p, bp, seq, window, 0, akv).reshape(bp, window, n_kv, hd))
        outs["pv"].append(tail_rows(u, bp, seq, window, v0, v1).reshape(bp, window, n_kv, hd))
        outs["pg"].append(sg_p)
        outs["pc"].append(tail_rows(uc, bp, seq, conv_w - 1, 0, width))
        outs["pl"].append(hl_p.reshape(bp, width))
        outs["sk"].append(jnp.concatenate([cache_swa_k[l][:, window - keep:], kr_s.reshape(bs, dseq, n_kv, hd)], 1))
        outs["sv"].append(jnp.concatenate([cache_swa_v[l][:, window - keep:],
                                           u[mp:, v0:v1].reshape(bs, dseq, n_kv, hd)], 1))
        outs["sg"].append(sg_s)
        outs["sc"].append(uc[mp:, :width].reshape(bs, dseq, width)[:, dseq - (conv_w - 1):])
        outs["sl"].append(hl_s.reshape(bs, width))

    yp, ys = _final_norm(x, y, ln_final, mp)
    st = {k: jnp.stack(v) for k, v in outs.items()}
    return (yp.reshape(bp, seq, d), ys.reshape(bs, dseq, d),
            st["pk"], st["pv"], st["pg"], st["pc"], st["pl"],
            st["sk"], st["sv"], st["sg"], st["sc"], st["sl"])
```

```python
import functools

import jax
import jax.numpy as jnp
from jax import lax
from jax.experimental import pallas as pl
from jax.experimental.pallas import tpu as pltpu

F32 = jnp.float32
BF16 = jnp.bfloat16

CHUNK = 64
PAST_LEN = 2048
TOP_K = 2
EPS = 1e-6
NEG_INF = -1e30
ROPE_THETA = 10000.0
GLA_TAU = 16.0
LRU_C = 8.0

LANES = 128
SUBLANES = 8
V7X_VMEM_BYTES = 64 * 1024 * 1024
MIB = 1024 * 1024
FFN_ROW_TILE = 1024
MOE_ROW_TILE = 768
FFN_DOWN_COLS = 512
FFN_SUB_ROWS = 256
MM_ROW_TILE = 2304


def _params(semantics, vmem_mib):
    assert vmem_mib * MIB < V7X_VMEM_BYTES
    return pltpu.CompilerParams(dimension_semantics=semantics, vmem_limit_bytes=vmem_mib * MIB)


def _tile(n, target):
    t = 1
    while t * 2 <= target and n % (t * 2) == 0:
        t *= 2
    return t


def _rope_table_kernel(cos_ref, sin_ref):
    rows, hd = cos_ref.shape
    half = hd // 2
    lane = lax.broadcasted_iota(jnp.int32, (rows, hd), 1)
    j = jnp.where(lane < half, lane, lane - half).astype(F32)
    inv = jnp.power(jnp.float32(ROPE_THETA), -j / half)
    pos = (lax.broadcasted_iota(jnp.int32, (rows, hd), 0) + pl.program_id(0) * rows).astype(F32)
    ang = pos * inv
    cos_ref[...] = jnp.cos(ang)
    s = jnp.sin(ang)
    sin_ref[...] = jnp.where(lane < half, -s, s)


def _rope_tables(n_pos, head_dim):
    rows = _tile(n_pos, 512)
    return pl.pallas_call(
        _rope_table_kernel,
        grid=(n_pos // rows,),
        out_specs=[pl.BlockSpec((rows, head_dim), lambda i: (i, 0))] * 2,
        out_shape=[jax.ShapeDtypeStruct((n_pos, head_dim), F32)] * 2,
        compiler_params=_params(("arbitrary",), 16),
        name="rope_tables",
    )()


def _rms(xf, g):
    return xf * lax.rsqrt(jnp.mean(xf * xf, axis=-1, keepdims=True) + EPS) * g


def _addnorm_kernel(*refs, has_y, emit_x):
    it = iter(refs)
    x_ref = next(it)
    y_ref = next(it) if has_y else None
    g_ref = next(it)
    xo_ref = next(it) if emit_x else None
    n_ref = next(it)
    x = x_ref[...]
    if has_y:
        x = x + y_ref[...]
    if emit_x:
        xo_ref[...] = x
    n_ref[...] = _rms(x, g_ref[...]).astype(n_ref.dtype)


def _addnorm(x, y, g, out_dtype):
    m, d = x.shape
    tm = _tile(m, 256)
    has_y = y is not None
    row = pl.BlockSpec((tm, d), lambda i: (i, 0))
    ins = [x] + ([y] if has_y else []) + [g.reshape(1, d)]
    in_specs = [row] * (2 if has_y else 1) + [pl.BlockSpec((1, d), lambda i: (0, 0))]
    out_shape = ([jax.ShapeDtypeStruct((m, d), F32)] if has_y else []) + [jax.ShapeDtypeStruct((m, d), out_dtype)]
    outs = pl.pallas_call(
        functools.partial(_addnorm_kernel, has_y=has_y, emit_x=has_y),
        grid=(m // tm,),
        in_specs=in_specs,
        out_specs=[row] * len(out_shape),
        out_shape=out_shape,
        compiler_params=_params(("arbitrary",), 48),
        name="add_rmsnorm",
    )(*ins)
    return (outs[0], outs[1]) if has_y else (x, outs[0])


def _final_norm_kernel(*refs, has_y, prompt_tiles):
    it = iter(refs)
    x_ref = next(it)
    y_ref = next(it) if has_y else None
    g_ref, p_ref, s_ref = next(it), next(it), next(it)
    i = pl.program_id(0)
    x = x_ref[...]
    if has_y:
        x = x + y_ref[...]
    n = _rms(x, g_ref[...])

    @pl.when(i < prompt_tiles)
    def _():
        p_ref[...] = n

    @pl.when(i >= prompt_tiles)
    def _():
        s_ref[...] = n


def _final_norm(x, y, g, mp):
    m, d = x.shape
    ms = m - mp
    tm = min(_tile(mp, 256), _tile(ms, 256))
    pt = mp // tm
    has_y = y is not None
    row = pl.BlockSpec((tm, d), lambda i: (i, 0))
    return pl.pallas_call(
        functools.partial(_final_norm_kernel, has_y=has_y, prompt_tiles=pt),
        grid=(m // tm,),
        in_specs=[row] * (2 if has_y else 1) + [pl.BlockSpec((1, d), lambda i: (0, 0))],
        out_specs=[pl.BlockSpec((tm, d), lambda i: (jnp.minimum(i, pt - 1), 0)),
                   pl.BlockSpec((tm, d), lambda i: (jnp.maximum(i - pt, 0), 0))],
        out_shape=[jax.ShapeDtypeStruct((mp, d), F32), jax.ShapeDtypeStruct((ms, d), F32)],
        compiler_params=_params(("arbitrary",), 48),
        name="final_rmsnorm",
    )(*([x] + ([y] if has_y else []) + [g.reshape(1, d)]))


def _mm_kernel(x_ref, w_ref, o_ref, *, shift):
    w = w_ref[...]
    if shift:
        w = pltpu.roll(w, w.shape[1] - shift, 1)[:, :o_ref.shape[1]]
    o_ref[...] = jnp.dot(x_ref[...], w.astype(BF16), preferred_element_type=F32).astype(o_ref.dtype)


def _mm(x, w, layer, col0, n, out_dtype=F32):
    m, k = x.shape
    tm = MM_ROW_TILE if m % MM_ROW_TILE == 0 else _tile(m, 1024)
    tn = _tile(n, 512)
    assert tn % LANES == 0
    shift = col0 % LANES
    if shift == 0:
        assert col0 % tn == 0
        w_spec = pl.BlockSpec((None, k, tn), lambda i, j: (layer, 0, j + col0 // tn))
    else:
        base = col0 - shift
        overhang = max(base + n + LANES - w.shape[2], 0)
        w_spec = pl.BlockSpec((None, pl.Element(k), pl.Element(tn + LANES, padding=(0, overhang))),
                              lambda i, j: (layer, 0, pl.multiple_of(base + j * tn, LANES)))
    return pl.pallas_call(
        functools.partial(_mm_kernel, shift=shift),
        grid=(m // tm, n // tn),
        in_specs=[pl.BlockSpec((tm, k), lambda i, j: (i, 0), pipeline_mode=pl.Buffered(1)), w_spec],
        out_specs=pl.BlockSpec((tm, tn), lambda i, j: (i, j)),
        out_shape=jax.ShapeDtypeStruct((m, n), out_dtype),
        compiler_params=_params(("arbitrary", "arbitrary"), 56),
        name="matmul",
    )(x, w)


def _log_sigmoid(z):
    y = -z
    return -(jnp.maximum(y, 0.0) + jnp.log1p(jnp.exp(-jnp.abs(y))))


def _lowrank_kernel(n_ref, w1_ref, w2_ref, b_ref, o_ref):
    glr = jnp.dot(n_ref[...], w1_ref[...], preferred_element_type=F32)
    z = jnp.dot(glr.astype(BF16), w2_ref[...], preferred_element_type=F32) + b_ref[...]
    o_ref[...] = _log_sigmoid(z) / GLA_TAU


def _lowrank_gate(n, w1, w2, b):
    m, d = n.shape
    gqk = w2.shape[1]
    tm = _tile(m, 512)
    return pl.pallas_call(
        _lowrank_kernel,
        grid=(m // tm,),
        in_specs=[
            pl.BlockSpec((tm, d), lambda i: (i, 0)),
            pl.BlockSpec(w1.shape, lambda i: (0, 0)),
            pl.BlockSpec(w2.shape, lambda i: (0, 0)),
            pl.BlockSpec((1, gqk), lambda i: (0, 0)),
        ],
        out_specs=pl.BlockSpec((tm, gqk), lambda i: (i, 0)),
        out_shape=jax.ShapeDtypeStruct((m, gqk), F32),
        compiler_params=_params(("arbitrary",), 32),
        name="gla_forget_gate",
    )(n, w1, w2, b)


def _rope(x, cos, sin):
    return x * cos + pltpu.roll(x, x.shape[-1] // 2, 1) * sin


def _swa_kernel(sink_ref, q_ref, k0_ref, k1_ref, k2_ref, v0_ref, v1_ref, v2_ref,
                c0_ref, c1_ref, c2_ref, s0_ref, s1_ref, s2_ref, o_ref, krot_ref, *, groups, cached):
    c = pl.program_id(1)
    hd = c2_ref.shape[-1]
    n_kv = k2_ref.shape[-1] // hd
    cosq, sinq = c2_ref[...], s2_ref[...]
    for kvh in range(n_kv):
        ks = slice(kvh * hd, (kvh + 1) * hd)
        k2 = _rope(k2_ref[:, ks], cosq, sinq)
        krot_ref[:, ks] = k2
        if cached:
            k0, k1 = k0_ref[:, ks], k1_ref[:, ks]
        else:
            k0 = _rope(k0_ref[:, ks], c0_ref[...], s0_ref[...])
            k1 = _rope(k1_ref[:, ks], c1_ref[...], s1_ref[...])
        kall = jnp.concatenate([k0, k1, k2], axis=0).astype(BF16)
        vall = jnp.concatenate([v0_ref[:, ks], v1_ref[:, ks], v2_ref[:, ks]], axis=0).astype(BF16)
        q0 = kvh * groups * hd
        qall = jnp.concatenate(
            [_rope(q_ref[:, q0 + g * hd:q0 + (g + 1) * hd], cosq, sinq) for g in range(groups)], axis=0).astype(BF16)
        s = lax.dot_general(qall, kall, (((1,), (1,)), ((), ())), preferred_element_type=F32) * hd ** -0.5
        if not cached:
            col = lax.broadcasted_iota(jnp.int32, s.shape, 1)
            s = jnp.where(col >= jnp.maximum(2 - c, 0) * CHUNK, s, NEG_INF)
        sink = jnp.concatenate(
            [jnp.full((CHUNK, 1), sink_ref[kvh * groups + g], F32) for g in range(groups)], axis=0)
        mx = jnp.maximum(jnp.max(s, axis=-1, keepdims=True), sink)
        p = jnp.exp(s - mx)
        den = jnp.sum(p, axis=-1, keepdims=True) + jnp.exp(sink - mx)
        p = p / den
        o = jnp.dot(p.astype(BF16), vall, preferred_element_type=F32)
        for g in range(groups):
            o_ref[:, q0 + g * hd:q0 + (g + 1) * hd] = o[g * CHUNK:(g + 1) * CHUNK].astype(o_ref.dtype)


def _swa(u, sink, cos, sin, cache_k, cache_v, *, n_seq, n_chunks, row0, n_kv, groups, hd, pos0, aq):
    cached = cache_k is not None
    akv = n_kv * hd
    assert aq % akv == 0
    rb0 = row0 // CHUNK
    pb0 = pos0 // CHUNK
    kcol, vcol = aq // akv, aq // akv + 1

    def rowblk(b, c):
        return rb0 + b * n_chunks + c

    def window(colblk):
        return [pl.BlockSpec((CHUNK, akv), lambda b, c, j=j: (rowblk(b, jnp.maximum(c - 2 + j, 0)), colblk))
                for j in range(2)]

    if cached:
        kprev = vprev = [pl.BlockSpec((CHUNK, akv), lambda b, c, j=j: (2 * b + j, 0)) for j in range(2)]
        kin, vin = [cache_k, cache_k], [cache_v, cache_v]
    else:
        kprev, vprev = window(kcol), window(vcol)
        kin = vin = [u, u]
    q_spec = pl.BlockSpec((CHUNK, aq), lambda b, c: (rowblk(b, c), 0))
    kcur = pl.BlockSpec((CHUNK, akv), lambda b, c: (rowblk(b, c), kcol))
    vcur = pl.BlockSpec((CHUNK, akv), lambda b, c: (rowblk(b, c), vcol))
    tab = [pl.BlockSpec((CHUNK, hd), lambda b, c, j=j: (pb0 + jnp.maximum(c - 2 + j, 0), 0)) for j in range(3)]
    out_rows = n_seq * n_chunks * CHUNK
    return pl.pallas_call(
        functools.partial(_swa_kernel, groups=groups, cached=cached),
        grid=(n_seq, n_chunks),
        in_specs=[pl.BlockSpec(memory_space=pltpu.SMEM), q_spec] + kprev + [kcur] + vprev + [vcur] + tab + tab,
        out_specs=[pl.BlockSpec((CHUNK, aq), lambda b, c: (b * n_chunks + c, 0)),
                   pl.BlockSpec((CHUNK, akv), lambda b, c: (b * n_chunks + c, 0))],
        out_shape=[jax.ShapeDtypeStruct((out_rows, aq), BF16), jax.ShapeDtypeStruct((out_rows, akv), F32)],
        compiler_params=_params(("arbitrary",) * 2, 32),
        name="swa_cached" if cached else "swa_prompt",
    )(sink, u, *kin, u, *vin, u, cos, cos, cos, sin, sin, sin)


def _cumsum_rows(x):
    row = lax.broadcasted_iota(jnp.int32, x.shape, 0)
    d = 1
    while d < x.shape[0]:
        x = x + jnp.where(row >= d, pltpu.roll(x, d, 0), 0.0)
        d *= 2
    return x


def _gla_kernel(*refs, heads, has_init):
    it = iter(refs)
    q_ref, k_ref, lg_ref, g_ref = (next(it) for _ in range(4))
    v_refs = [next(it) for _ in range(heads)]
    r_refs = [next(it) for _ in range(heads)]
    s0_ref = next(it) if has_init else None
    o_ref, sout_ref, st_ref = next(it), next(it), next(it)
    c = pl.program_id(1)
    dk = q_ref.shape[-1] // heads
    dv = v_refs[0].shape[-1]

    @pl.when(c == 0)
    def _():
        for h in range(heads):
            st_ref[h] = s0_ref[h].T if has_init else jnp.zeros((dv, dk), F32)

    tri = lax.broadcasted_iota(jnp.int32, (CHUNK, CHUNK), 0) >= lax.broadcasted_iota(jnp.int32, (CHUNK, CHUNK), 1)
    for h in range(heads):
        ks = slice(h * dk, (h + 1) * dk)
        b = _cumsum_rows(lg_ref[:, ks])
        b_last = b[CHUNK - 1:CHUNK, :]
        k = k_ref[:, ks]
        qe = (q_ref[:, ks] * dk ** -0.5 * jnp.exp(b)).astype(BF16)
        ke = (k * jnp.exp(-b)).astype(BF16)
        kd = (k * jnp.exp(b_last - b)).astype(BF16)
        v = v_refs[h][...].astype(BF16)
        att = lax.dot_general(qe, ke, (((1,), (1,)), ((), ())), preferred_element_type=F32)
        att = jnp.where(tri, att, 0.0)
        st = st_ref[h]
        o = lax.dot_general(qe, st.astype(BF16), (((1,), (1,)), ((), ())), preferred_element_type=F32)
        o = o + jnp.dot(att.astype(BF16), v, preferred_element_type=F32)
        st_new = jnp.exp(b_last) * st + lax.dot_general(v, kd, (((0,), (0,)), ((), ())), preferred_element_type=F32)
        st_ref[h] = st_new
        r = r_refs[h][...]
        o_ref[:, h * dv:(h + 1) * dv] = (_rms(o, g_ref[...]) * (r * jax.nn.sigmoid(r))).astype(o_ref.dtype)

    @pl.when(c == pl.num_programs(1) - 1)
    def _():
        for h in range(heads):
            sout_ref[h] = st_ref[h].T


def _gla(u, lg, g_norm, s0, *, n_seq, n_chunks, row0, heads, dk, dv, q_col, k_col, v_col, r_col):
    has_init = s0 is not None
    rb0 = row0 // CHUNK
    gqk = heads * dk
    assert q_col % gqk == 0 and k_col % gqk == 0 and v_col % dv == 0 and r_col % dv == 0

    def col(width, blk):
        return pl.BlockSpec((CHUNK, width), lambda b, c: (rb0 + b * n_chunks + c, blk))

    in_specs = ([col(gqk, q_col // gqk), col(gqk, k_col // gqk), col(gqk, 0), pl.BlockSpec((1, dv), lambda b, c: (0, 0))]
                + [col(dv, v_col // dv + h) for h in range(heads)] + [col(dv, r_col // dv + h) for h in range(heads)])
    ins = [u, u, lg, g_norm.reshape(1, dv)] + [u] * (2 * heads)
    if has_init:
        in_specs.append(pl.BlockSpec((None, heads, dk, dv), lambda b, c: (b, 0, 0, 0)))
        ins.append(s0)
    out_rows = n_seq * n_chunks * CHUNK
    return pl.pallas_call(
        functools.partial(_gla_kernel, heads=heads, has_init=has_init),
        grid=(n_seq, n_chunks),
        in_specs=in_specs,
        out_specs=[pl.BlockSpec((CHUNK, heads * dv), lambda b, c: (b * n_chunks + c, 0)),
                   pl.BlockSpec((None, heads, dk, dv), lambda b, c: (b, 0, 0, 0))],
        out_shape=[jax.ShapeDtypeStruct((out_rows, heads * dv), BF16),
                   jax.ShapeDtypeStruct((n_seq, heads, dk, dv), F32)],
        scratch_shapes=[pltpu.VMEM((heads, dv, dk), F32)],
        compiler_params=_params(("arbitrary",) * 2, 40),
        name="gla_init" if has_init else "gla_prompt",
    )(*ins)


def _softplus(y):
    return jnp.maximum(y, 0.0) + jnp.log1p(jnp.exp(-jnp.abs(y)))


def _lru_kernel(x_ref, xprev_ref, cprev_ref, y_ref, wc_ref, bc_ref, wa_ref, ba_ref, wx_ref, bx_ref, lam_ref, h0_ref,
                o_ref, hl_ref, h_ref, a_ref, u_ref, *, stream_start, conv_w):
    c = pl.program_id(1)
    t, w = x_ref.shape
    nblk, bw = wa_ref.shape[0], wa_ref.shape[1]

    @pl.when(c == 0)
    def _():
        h_ref[...] = h0_ref[...]

    prev = jnp.where(c == 0, cprev_ref[...], xprev_ref[...])
    ext = jnp.concatenate([prev, x_ref[...]], axis=0)
    xc = bc_ref[...]
    for j in range(conv_w):
        shift = conv_w - 1 - j
        xs = ext if shift == 0 else pltpu.roll(ext, shift, 0)
        xc = xc + xs[SUBLANES:] * wc_ref[j:j + 1, :]
    xb = xc.astype(BF16)
    ra = jnp.concatenate([jnp.dot(xb[:, n * bw:(n + 1) * bw], wa_ref[n], preferred_element_type=F32)
                          for n in range(nblk)], axis=1)
    ia = jnp.concatenate([jnp.dot(xb[:, n * bw:(n + 1) * bw], wx_ref[n], preferred_element_type=F32)
                          for n in range(nblk)], axis=1)
    r = jax.nn.sigmoid(ra + ba_ref[...])
    gate = jax.nn.sigmoid(ia + bx_ref[...])
    log_a = -LRU_C * r * _softplus(-lam_ref[...])
    a = jnp.exp(log_a)
    mult = jnp.sqrt(jnp.tanh(-log_a) * (a * a + 1.0))
    if stream_start:
        first = (lax.broadcasted_iota(jnp.int32, (t, w), 0) == 0) & (c == 0)
        mult = jnp.where(first, 1.0, mult)
        a = jnp.where(first, 0.0, a)
    a_ref[...] = a
    u_ref[...] = mult * gate * xc

    def step(i, h):
        h = a_ref[pl.ds(i, 1), :] * h + u_ref[pl.ds(i, 1), :]
        u_ref[pl.ds(i, 1), :] = h
        return h

    h = lax.fori_loop(0, t, step, h_ref[...], unroll=8)
    h_ref[...] = h
    hl_ref[...] = h
    o_ref[...] = (jax.nn.gelu(y_ref[...]) * u_ref[...]).astype(o_ref.dtype)


def _lru(uc, cprev8, h0, wc, bc, wa, ba, wx, bx, lam, *, n_seq, n_chunks, row0, width, stream_start):
    rb0 = row0 // CHUNK
    per = CHUNK // SUBLANES
    conv_w = wc.shape[0]
    vec = pl.BlockSpec((1, width), lambda b, c: (0, 0))
    blk = pl.BlockSpec(wa.shape, lambda b, c: (0, 0, 0))
    out_rows = n_seq * n_chunks * CHUNK
    return pl.pallas_call(
        functools.partial(_lru_kernel, stream_start=stream_start, conv_w=conv_w),
        grid=(n_seq, n_chunks),
        in_specs=[
            pl.BlockSpec((CHUNK, width), lambda b, c: (rb0 + b * n_chunks + c, 0)),
            pl.BlockSpec((SUBLANES, width), lambda b, c: (jnp.maximum((rb0 + b * n_chunks + c) * per - 1, 0), 0)),
            pl.BlockSpec((None, SUBLANES, width), lambda b, c: (b, 0, 0)),
            pl.BlockSpec((CHUNK, width), lambda b, c: (rb0 + b * n_chunks + c, 1)),
            pl.BlockSpec(wc.shape, lambda b, c: (0, 0)), vec, blk, vec, blk, vec, vec,
            pl.BlockSpec((None, 1, width), lambda b, c: (b, 0, 0)),
        ],
        out_specs=[pl.BlockSpec((CHUNK, width), lambda b, c: (b * n_chunks + c, 0)),
                   pl.BlockSpec((None, 1, width), lambda b, c: (b, 0, 0))],
        out_shape=[jax.ShapeDtypeStruct((out_rows, width), BF16), jax.ShapeDtypeStruct((n_seq, 1, width), F32)],
        scratch_shapes=[pltpu.VMEM((1, width), F32), pltpu.VMEM((CHUNK, width), F32), pltpu.VMEM((CHUNK, width), F32)],
        compiler_params=_params(("arbitrary",) * 2, 32),
        name="conv_rglru",
    )(uc, uc, cprev8, uc, wc, bc, wa, ba, wx, bx, lam, h0)


def _merge_kernel(n_ref, ap_ref, bp_ref, cp_ref, as_ref, bs_ref, cs_ref, wg0_ref, wg1_ref, wg2_ref, wb_ref,
                  bg0_ref, bg1_ref, bg2_ref, o_ref, wgbf_ref, wbbf_ref, *, prompt_tiles):
    i = pl.program_id(1)

    @pl.when(i == 0)
    def _():
        for b, wg_ref in enumerate((wg0_ref, wg1_ref, wg2_ref)):
            wgbf_ref[b] = wg_ref[...].astype(BF16)
            wbbf_ref[b] = wb_ref[b].astype(BF16)

    def merge(branch_refs):
        n = n_ref[...]
        acc = None
        for b, (br_ref, bg_ref) in enumerate(zip(branch_refs, (bg0_ref, bg1_ref, bg2_ref))):
            gate = jax.nn.sigmoid(jnp.dot(n, wgbf_ref[b], preferred_element_type=F32) + bg_ref[...])
            term = gate * jnp.dot(br_ref[...], wbbf_ref[b], preferred_element_type=F32)
            acc = term if acc is None else acc + term
        o_ref[...] = acc.astype(o_ref.dtype)

    @pl.when(i < prompt_tiles)
    def _():
        merge((ap_ref, bp_ref, cp_ref))

    @pl.when(i >= prompt_tiles)
    def _():
        merge((as_ref, bs_ref, cs_ref))


def _merge(n, prompt, sample, w_gate, b_gate, w_branch, layer):
    m, d = n.shape
    mp, bw = prompt[0].shape
    ms = sample[0].shape[0]
    nb = w_branch.shape[1]
    assert nb == 3 and mp + ms == m
    tm = min(_tile(mp, 512), _tile(ms, 512))
    tn = _tile(d, 256)
    nj = d // tn
    pt = mp // tm
    single = pl.Buffered(1)
    wg_specs = [pl.BlockSpec((None, d, tn), lambda j, i, b=b: (layer, 0, b * nj + j), pipeline_mode=single)
                for b in range(nb)]
    bg_specs = [pl.BlockSpec((None, 1, tn), lambda j, i, b=b: (layer, 0, b * nj + j)) for b in range(nb)]
    p_spec = pl.BlockSpec((tm, bw), lambda j, i: (jnp.minimum(i, pt - 1), 0))
    s_spec = pl.BlockSpec((tm, bw), lambda j, i: (jnp.maximum(i - pt, 0), 0), pipeline_mode=single)
    return pl.pallas_call(
        functools.partial(_merge_kernel, prompt_tiles=pt),
        grid=(nj, m // tm),
        in_specs=[pl.BlockSpec((tm, d), lambda j, i: (i, 0))] + [p_spec] * nb + [s_spec] * nb + wg_specs
        + [pl.BlockSpec((None, nb, bw, tn), lambda j, i: (layer, 0, 0, j), pipeline_mode=single)] + bg_specs,
        out_specs=pl.BlockSpec((tm, tn), lambda j, i: (i, j)),
        out_shape=jax.ShapeDtypeStruct((m, d), BF16),
        scratch_shapes=[pltpu.VMEM((nb, d, tn), BF16), pltpu.VMEM((nb, bw, tn), BF16)],
        compiler_params=_params(("arbitrary", "arbitrary"), 60),
        name="gated_merge",
    )(n, *prompt, *sample, w_gate, w_gate, w_gate, w_branch, b_gate, b_gate, b_gate)


def _ffn_kernel(te_ref, nv_ref, ns_ref, x_ref, wg_ref, wu_ref, wd_ref, o_ref, *bf_refs, sub):
    i, j = pl.program_id(0), pl.program_id(1)
    tm, d = o_ref.shape
    dc = min(d, FFN_DOWN_COLS)
    nsub = tm // sub
    used = ns_ref[i]

    def rows_body(rows, wg, wu, wd_cols):
        x = x_ref[rows, :]
        g = jnp.dot(x, wg, preferred_element_type=F32)
        u = jnp.dot(x, wu, preferred_element_type=F32)
        h = (g * jax.nn.sigmoid(g) * u).astype(BF16)

        @pl.when(j == 0)
        def _():
            for n0 in range(0, d, dc):
                o_ref[rows, n0:n0 + dc] = jnp.dot(h, wd_cols(n0), preferred_element_type=F32)

        @pl.when(j > 0)
        def _():
            for n0 in range(0, d, dc):
                o_ref[rows, n0:n0 + dc] += jnp.dot(h, wd_cols(n0), preferred_element_type=F32)

    @pl.when(used == nsub)
    def _():
        rows_body(slice(0, tm), wg_ref[...].astype(BF16), wu_ref[...].astype(BF16),
                  lambda n0: wd_ref[:, n0:n0 + dc].astype(BF16))

    if nsub > 1:
        wgbf_ref, wubf_ref, wdbf_ref = bf_refs

        @pl.when((used > 0) & (used < nsub))
        def _():
            wg, wu, wd = wg_ref[...].astype(BF16), wu_ref[...].astype(BF16), wd_ref[...].astype(BF16)
            wgbf_ref[...] = wg
            wubf_ref[...] = wu
            wdbf_ref[...] = wd
            rows_body(slice(0, sub), wg, wu, lambda n0: wd[:, n0:n0 + dc])
            for s in range(1, nsub - 1):
                @pl.when(s < used)
                def _(s=s):
                    rows_body(slice(s * sub, (s + 1) * sub), wgbf_ref[...], wubf_ref[...],
                              lambda n0: wdbf_ref[:, n0:n0 + dc])

    for s in range(nsub):
        @pl.when((s >= used) & (j == 0))
        def _(s=s):
            o_ref[s * sub:(s + 1) * sub, :] = jnp.zeros((sub, d), o_ref.dtype)


def _ffn(x, wg, wu, wd, tile_expert, n_valid, n_sub, tm, sub):
    r, d = x.shape
    f = wg.shape[2]
    tf = _tile(f, 256)
    nj = f // tf
    single = pl.Buffered(1)

    def last(i, nv):
        return jnp.maximum(jnp.minimum(i, nv[0] - 1), 0)

    def row(i, j, te, nv, ns):
        return (last(i, nv), 0)

    def fidx(i, j, nv):
        return jnp.where(i < nv[0], j, nj - 1)

    def up(i, j, te, nv, ns):
        return (te[last(i, nv)], 0, fidx(i, j, nv))

    def down(i, j, te, nv, ns):
        return (te[last(i, nv)], fidx(i, j, nv), 0)

    return pl.pallas_call(
        functools.partial(_ffn_kernel, sub=sub),
        grid_spec=pltpu.PrefetchScalarGridSpec(
            num_scalar_prefetch=3,
            grid=(r // tm, nj),
            in_specs=[pl.BlockSpec((tm, d), row, pipeline_mode=single),
                      pl.BlockSpec((None, d, tf), up), pl.BlockSpec((None, d, tf), up),
                      pl.BlockSpec((None, tf, d), down)],
            out_specs=pl.BlockSpec((tm, d), lambda i, j, te, nv, ns: (i, 0), pipeline_mode=single),
            scratch_shapes=([pltpu.VMEM((d, tf), BF16), pltpu.VMEM((d, tf), BF16), pltpu.VMEM((tf, d), BF16)]
                            if sub < tm else []),
        ),
        out_shape=jax.ShapeDtypeStruct((r, d), F32),
        compiler_params=_params(("arbitrary", "arbitrary"), 60),
        name="swiglu_ffn",
    )(tile_expert, n_valid, n_sub, x, wg, wu, wd)


def _router_kernel(x_ref, g_ref, w_ref, meta_ref, cnt_ref, carry_ref, *, n_experts):
    @pl.when(pl.program_id(0) == 0)
    def _():
        carry_ref[...] = jnp.zeros_like(carry_ref)

    tm = x_ref.shape[0]
    n = _rms(x_ref[...], g_ref[...])
    logits = jnp.dot(n, w_ref[...], preferred_element_type=F32, precision=lax.Precision.HIGHEST)
    lane = lax.broadcasted_iota(jnp.int32, logits.shape, 1).astype(F32)
    logits = jnp.where(lane < n_experts, logits, -jnp.inf)
    m1 = jnp.max(logits, axis=-1, keepdims=True)
    i1 = jnp.min(jnp.where(logits == m1, lane, float(LANES)), axis=-1, keepdims=True)
    rest = jnp.where(lane == i1, -jnp.inf, logits)
    m2 = jnp.max(rest, axis=-1, keepdims=True)
    i2 = jnp.min(jnp.where(rest == m2, lane, float(LANES)), axis=-1, keepdims=True)
    e2 = jnp.exp(m2 - m1)
    den = 1.0 + e2
    oh1 = lane == i1
    oh2 = lane == i2
    both = jnp.where(oh1 | oh2, 1.0, 0.0)
    rr = lax.broadcasted_iota(jnp.int32, (tm, tm), 0)
    cc = lax.broadcasted_iota(jnp.int32, (tm, tm), 1)
    below = jnp.where(rr > cc, 1.0, 0.0).astype(BF16)
    before = jnp.dot(below, both.astype(BF16), preferred_element_type=F32) + carry_ref[...]
    rank1 = jnp.sum(jnp.where(oh1, before, 0.0), axis=-1, keepdims=True)
    rank2 = jnp.sum(jnp.where(oh2, before, 0.0), axis=-1, keepdims=True)
    cols = (i1, i2, 1.0 / den, e2 / den, rank1, rank2)
    meta = jnp.zeros(logits.shape, F32)
    for idx, val in enumerate(cols):
        meta = jnp.where(lane == idx, val, meta)
    meta_ref[...] = meta
    total = carry_ref[...] + jnp.sum(both, axis=0, keepdims=True)
    carry_ref[...] = total
    cnt_ref[...] = total


def _router(x, g, w_pad, n_experts):
    m, d = x.shape
    tm = _tile(m, 256)
    return pl.pallas_call(
        functools.partial(_router_kernel, n_experts=n_experts),
        grid=(m // tm,),
        in_specs=[pl.BlockSpec((tm, d), lambda i: (i, 0)), pl.BlockSpec((1, d), lambda i: (0, 0)),
                  pl.BlockSpec(w_pad.shape, lambda i: (0, 0))],
        out_specs=[pl.BlockSpec((tm, LANES), lambda i: (i, 0)), pl.BlockSpec((1, LANES), lambda i: (0, 0))],
        out_shape=[jax.ShapeDtypeStruct((m, LANES), F32), jax.ShapeDtypeStruct((1, LANES), F32)],
        scratch_shapes=[pltpu.VMEM((1, LANES), F32)],
        compiler_params=_params(("arbitrary",), 40),
        name="moe_router",
    )(x, g.reshape(1, d), w_pad)


def _dispatch_kernel(used_ref, src_ref, nxt_ref, x_hbm, g_ref, o_ref, buf_ref, sem):
    i = pl.program_id(0)
    rows = o_ref.shape[0]
    slot = i % 2

    def copy(tbl_ref, r, s):
        return pltpu.make_async_copy(x_hbm.at[pl.ds(tbl_ref[0, r], 1)], buf_ref.at[s, pl.ds(r, 1)], sem.at[s])

    def start_all(tbl_ref, s):
        def body(r, _):
            copy(tbl_ref, r, s).start()
            return 0
        lax.fori_loop(0, rows, body, 0)

    def in_use(step):
        return step * rows < used_ref[0]

    @pl.when((i == 0) & in_use(0))
    def _():
        start_all(src_ref, 0)

    @pl.when((i + 1 < pl.num_programs(0)) & in_use(i + 1))
    def _():
        start_all(nxt_ref, 1 - slot)

    @pl.when(in_use(i))
    def _():
        def body(r, _):
            copy(src_ref, r, slot).wait()
            return 0
        lax.fori_loop(0, rows, body, 0)
        o_ref[...] = _rms(buf_ref[slot], g_ref[...]).astype(o_ref.dtype)

    @pl.when(jnp.logical_not(in_use(i)))
    def _():
        o_ref[...] = jnp.zeros_like(o_ref)


def _dispatch(x, g, src, n_used):
    m, d = x.shape
    n_rows = src.shape[0]
    rows = _tile(n_rows, 256)
    steps = n_rows // rows
    src3 = src.reshape(steps, 1, rows)
    return pl.pallas_call(
        _dispatch_kernel,
        grid=(steps,),
        in_specs=[pl.BlockSpec(memory_space=pltpu.SMEM),
                  pl.BlockSpec((None, 1, rows), lambda i: (i, 0, 0), memory_space=pltpu.SMEM),
                  pl.BlockSpec((None, 1, rows), lambda i: (jnp.minimum(i + 1, steps - 1), 0, 0),
                               memory_space=pltpu.SMEM),
                  pl.BlockSpec(memory_space=pl.ANY), pl.BlockSpec((1, d), lambda i: (0, 0))],
        out_specs=pl.BlockSpec((rows, d), lambda i: (i, 0)),
        out_shape=jax.ShapeDtypeStruct((n_rows, d), BF16),
        scratch_shapes=[pltpu.VMEM((2, rows, d), F32), pltpu.SemaphoreType.DMA((2,))],
        compiler_params=_params(("arbitrary",), 32),
        name="moe_dispatch",
    )(n_used, src3, src3, x, g.reshape(1, d))


def _combine_kernel(dest_ref, x_ref, meta_ref, y_hbm, o_ref, ya_ref, yb_ref, sem, *, rows):
    bufs = (ya_ref, yb_ref)

    def copy(r, k):
        return pltpu.make_async_copy(y_hbm.at[pl.ds(dest_ref[0, TOP_K * r + k], 1)], bufs[k].at[pl.ds(r, 1)], sem)

    def start(r, _):
        for k in range(TOP_K):
            copy(r, k).start()
        return 0

    def wait(r, _):
        for k in range(TOP_K):
            copy(r, k).wait()
        return 0

    lax.fori_loop(0, rows, start, 0)
    lax.fori_loop(0, rows, wait, 0)
    meta = meta_ref[...]
    o_ref[...] = x_ref[...] + (meta[:, 2:3] * ya_ref[...] + meta[:, 3:4] * yb_ref[...])


def _combine(x, meta, y, dest):
    m, d = x.shape
    rows = _tile(m, 128)
    dest3 = dest.reshape(m // rows, 1, rows * TOP_K)
    return pl.pallas_call(
        functools.partial(_combine_kernel, rows=rows),
        grid=(m // rows,),
        in_specs=[pl.BlockSpec((None, 1, rows * TOP_K), lambda i: (i, 0, 0), memory_space=pltpu.SMEM),
                  pl.BlockSpec((rows, d), lambda i: (i, 0)), pl.BlockSpec((rows, LANES), lambda i: (i, 0)),
                  pl.BlockSpec(memory_space=pl.ANY)],
        out_specs=pl.BlockSpec((rows, d), lambda i: (i, 0)),
        out_shape=jax.ShapeDtypeStruct((m, d), F32),
        scratch_shapes=[pltpu.VMEM((rows, d), F32), pltpu.VMEM((rows, d), F32), pltpu.SemaphoreType.DMA(())],
        compiler_params=_params(("arbitrary",), 32),
        name="moe_combine",
    )(dest3, x, meta, y)


def _moe(x, g, w_router, wg, wu, wd, tm):
    m, d = x.shape
    n_exp = w_router.shape[1]
    w_pad = jnp.zeros((d, LANES), F32).at[:, :n_exp].set(w_router)
    meta, counts = _router(x, g, w_pad, n_exp)
    expert = meta[:, 0:TOP_K].astype(jnp.int32)
    rank = meta[:, 4:4 + TOP_K].astype(jnp.int32)
    cnt = counts[0, :n_exp].astype(jnp.int32)
    padded = (cnt + tm - 1) // tm * tm
    ends = jnp.cumsum(padded)
    dest = (ends - padded)[expert] + rank
    n_tiles = (m * TOP_K) // tm + n_exp
    tile_ids = jnp.arange(n_tiles, dtype=jnp.int32)
    tile_expert = jnp.minimum(jnp.sum((ends // tm)[None, :] <= tile_ids[:, None], axis=1), n_exp - 1).astype(jnp.int32)
    n_valid = (ends[-1:] // tm).astype(jnp.int32)
    sub = FFN_SUB_ROWS if tm % FFN_SUB_ROWS == 0 else tm
    used = jnp.clip(cnt[tile_expert] - (tile_ids - ((ends - padded) // tm)[tile_expert]) * tm, 0, tm)
    n_sub = jnp.where(tile_ids < n_valid[0], (used + sub - 1) // sub, 0).astype(jnp.int32)
    token = jnp.broadcast_to(jnp.arange(m, dtype=jnp.int32)[:, None], (m, TOP_K))
    src = (jnp.arange(n_tiles * tm, dtype=jnp.int32) % m).at[dest.reshape(-1)].set(token.reshape(-1))
    ns = _dispatch(x, g, src, ends[-1:].astype(jnp.int32))
    ys = _ffn(ns, wg, wu, wd, tile_expert, n_valid, n_sub, tm, sub)
    return _combine(x, meta, ys, dest)


def kernel(x_prompt, x_sample, cache_swa_k, cache_swa_v, state_gla, state_conv, state_lru, ln_mix, w_in, attn_sink, w_gla_lr2, b_gla_gate, g_gla_norm, w_conv, b_conv, w_lru_a, b_lru_a, w_lru_x, b_lru_x, lru_lambda, w_branch, w_gate, b_gate, w_out, ln_ffn, w_ff_gate, w_ff_up, w_ff_down, w_router, w_moe_gate, w_moe_up, w_moe_down, ln_final):
    bp, seq, d = x_prompt.shape
    bs, dseq, _ = x_sample.shape
    depth = ln_mix.shape[0]
    _, _, window, n_kv, hd = cache_swa_k.shape
    n_heads = attn_sink.shape[1]
    groups = n_heads // n_kv
    _, _, gh, dk, dv = state_gla.shape
    lowrank = w_gla_lr2.shape[1]
    conv_w = w_conv.shape[1]
    width = state_lru.shape[2]
    aq, akv, gqk, gv = n_heads * hd, n_kv * hd, gh * dk, gh * dv
    assert dseq == CHUNK and seq % CHUNK == 0 and window == 2 * CHUNK and hd == LANES and lowrank <= LANES
    mp, ms = bp * seq, bs * dseq
    m = mp + ms
    nc = seq // CHUNK
    main = aq + 2 * akv + 2 * gqk + 2 * gv
    c_col = main + lowrank
    assert w_in.shape[2] == c_col + 2 * width

    w_lr = jnp.zeros((depth, d, LANES), BF16).at[:, :, :lowrank].set(w_in[:, :, main:c_col].astype(BF16))
    w_lr2 = jnp.zeros((depth, LANES, gqk), BF16).at[:, :lowrank].set(w_gla_lr2.astype(BF16))
    cache_k2 = cache_swa_k.reshape(depth, bs * window, akv)
    cache_v2 = cache_swa_v.reshape(depth, bs * window, akv)
    conv_prev_s = jnp.pad(state_conv, ((0, 0), (0, 0), (SUBLANES - (conv_w - 1), 0), (0, 0)))
    conv_prev_p = jnp.zeros((bp, SUBLANES, width), F32)
    wa_bf, wx_bf = w_lru_a.astype(BF16), w_lru_x.astype(BF16)

    n_pos = max(seq, PAST_LEN + dseq)
    cos, sin = _rope_tables(n_pos, hd)

    x = jnp.concatenate([x_prompt.reshape(mp, d), x_sample.reshape(ms, d)], axis=0)
    y = None
    outs = {k: [] for k in ("pk", "pv", "pg", "pc", "pl", "sk", "sv", "sg", "sc", "sl")}
    tm_ffn = FFN_ROW_TILE if m % FFN_ROW_TILE == 0 else _tile(m, FFN_ROW_TILE)
    tm_moe = MOE_ROW_TILE if (m * TOP_K) % MOE_ROW_TILE == 0 else _tile(m, MOE_ROW_TILE)
    for l in range(depth):
        x, n = _addnorm(x, y, ln_mix[l], BF16)
        u = _mm(n, w_in, l, 0, main)
        uc = _mm(n, w_in, l, c_col, 2 * width)
        lg = _lowrank_gate(n, w_lr[l], w_lr2[l], b_gla_gate[l].reshape(1, gqk))

        swa = functools.partial(_swa, u, attn_sink[l], cos, sin, n_kv=n_kv, groups=groups, hd=hd, aq=aq)
        oa_p, kr_p = swa(None, None, n_seq=bp, n_chunks=nc, row0=0, pos0=0)
        oa_s, kr_s = swa(cache_k2[l], cache_v2[l], n_seq=bs, n_chunks=1, row0=mp, pos0=PAST_LEN)

        gla = functools.partial(_gla, u, lg, g_gla_norm[l], heads=gh, dk=dk, dv=dv, q_col=aq + 2 * akv,
                                k_col=aq + 2 * akv + gqk, v_col=aq + 2 * akv + 2 * gqk,
                                r_col=aq + 2 * akv + 2 * gqk + gv)
        ob_p, sg_p = gla(None, n_seq=bp, n_chunks=nc, row0=0)
        ob_s, sg_s = gla(state_gla[l], n_seq=bs, n_chunks=1, row0=mp)

        lru = functools.partial(_lru, uc, wc=w_conv[l], bc=b_conv[l].reshape(1, width), wa=wa_bf[l],
                                ba=b_lru_a[l].reshape(1, width), wx=wx_bf[l], bx=b_lru_x[l].reshape(1, width),
                                lam=lru_lambda[l].reshape(1, width), width=width)
        oc_p, hl_p = lru(conv_prev_p, jnp.zeros((bp, 1, width), F32), n_seq=bp, n_chunks=nc, row0=0,
                         stream_start=True)
        oc_s, hl_s = lru(conv_prev_s[l], state_lru[l].reshape(bs, 1, width), n_seq=bs, n_chunks=1, row0=mp,
                         stream_start=False)

        merged = _merge(n, (oa_p, ob_p, oc_p), (oa_s, ob_s, oc_s), w_gate, b_gate.reshape(depth, 1, -1), w_branch, l)
        y = _mm(merged, w_out, l, 0, d)
        x, n2 = _addnorm(x, y, ln_ffn[l], BF16)
        if l % 2 == 0:
            i = l // 2
            tiles = m // tm_ffn
            y = _ffn(n2, w_ff_gate[i:i + 1], w_ff_up[i:i + 1], w_ff_down[i:i + 1], jnp.zeros((tiles,), jnp.int32),
                     jnp.full((1,), tiles, jnp.int32), jnp.ones((tiles,), jnp.int32), tm_ffn, tm_ffn)
        else:
            i = l // 2
            x = _moe(x, ln_ffn[l], w_router[i], w_moe_gate[i], w_moe_up[i], w_moe_down[i], tm_moe)
            y = None

        def tail_rows(arr, b, t, keep_rows, c0, c1):
            return jnp.stack([arr[i * t + t - keep_rows:(i + 1) * t, c0:c1] for i in range(b)])

        v0, v1 = aq + akv, aq + 2 * akv
        keep = window - dseq
        outs["pk"].append(tail_rows(kr_p, bp, seq, window, 0, akv).reshape(bp, window, n_kv, hd))
        outs["pv"].append(tail_rows(u, bp, seq, window, v0, v1).reshape(bp, window, n_kv, hd))
        outs["pg"].append(sg_p)
        outs["pc"].append(tail_rows(uc, bp, seq, conv_w - 1, 0, width))
        outs["pl"].append(hl_p.reshape(bp, width))
        outs["sk"].append(jnp.concatenate([cache_swa_k[l][:, window - keep:], kr_s.reshape(bs, dseq, n_kv, hd)], 1))
        outs["sv"].append(jnp.concatenate([cache_swa_v[l][:, window - keep:],
                                           u[mp:, v0:v1].reshape(bs, dseq, n_kv, hd)], 1))
        outs["sg"].append(sg_s)
        outs["sc"].append(uc[mp:, :width].reshape(bs, dseq, width)[:, dseq - (conv_w - 1):])
        outs["sl"].append(hl_s.reshape(bs, width))

    yp, ys = _final_norm(x, y, ln_final, mp)
    st = {k: jnp.stack(v) for k, v in outs.items()}
    return (yp.reshape(bp, seq, d), ys.reshape(bs, dseq, d),
            st["pk"], st["pv"], st["pg"], st["pc"], st["pl"],
            st["sk"], st["sv"], st["sg"], st["sc"], st["sl"])
```

```python
import functools

import jax
import jax.numpy as jnp
from jax import lax
from jax.experimental import pallas as pl
from jax.experimental.pallas import tpu as pltpu

F32 = jnp.float32
BF16 = jnp.bfloat16

CHUNK = 64
PAST_LEN = 2048
TOP_K = 2
EPS = 1e-6
NEG_INF = -1e30
ROPE_THETA = 10000.0
GLA_TAU = 16.0
LRU_C = 8.0

LANES = 128
SUBLANES = 8
V7X_VMEM_BYTES = 64 * 1024 * 1024
MIB = 1024 * 1024
FFN_ROW_TILE = 1024
MOE_ROW_TILE = 768
FFN_DOWN_COLS = 512
FFN_SUB_ROWS = 256
MM_ROW_TILE = 2304


def _params(semantics, vmem_mib):
    assert vmem_mib * MIB < V7X_VMEM_BYTES
    return pltpu.CompilerParams(dimension_semantics=semantics, vmem_limit_bytes=vmem_mib * MIB)


def _tile(n, target):
    t = 1
    while t * 2 <= target and n % (t * 2) == 0:
        t *= 2
    return t


def _rope_table_kernel(cos_ref, sin_ref):
    rows, hd = cos_ref.shape
    half = hd // 2
    lane = lax.broadcasted_iota(jnp.int32, (rows, hd), 1)
    j = jnp.where(lane < half, lane, lane - half).astype(F32)
    inv = jnp.power(jnp.float32(ROPE_THETA), -j / half)
    pos = (lax.broadcasted_iota(jnp.int32, (rows, hd), 0) + pl.program_id(0) * rows).astype(F32)
    ang = pos * inv
    cos_ref[...] = jnp.cos(ang)
    s = jnp.sin(ang)
    sin_ref[...] = jnp.where(lane < half, -s, s)


def _rope_tables(n_pos, head_dim):
    rows = _tile(n_pos, 512)
    return pl.pallas_call(
        _rope_table_kernel,
        grid=(n_pos // rows,),
        out_specs=[pl.BlockSpec((rows, head_dim), lambda i: (i, 0))] * 2,
        out_shape=[jax.ShapeDtypeStruct((n_pos, head_dim), F32)] * 2,
        compiler_params=_params(("arbitrary",), 16),
        name="rope_tables",
    )()


def _rms(xf, g):
    return xf * lax.rsqrt(jnp.mean(xf * xf, axis=-1, keepdims=True) + EPS) * g


def _addnorm_kernel(*refs, has_y, emit_x):
    it = iter(refs)
    x_ref = next(it)
    y_ref = next(it) if has_y else None
    g_ref = next(it)
    xo_ref = next(it) if emit_x else None
    n_ref = next(it)
    x = x_ref[...]
    if has_y:
        x = x + y_ref[...]
    if emit_x:
        xo_ref[...] = x
    n_ref[...] = _rms(x, g_ref[...]).astype(n_ref.dtype)


def _addnorm(x, y, g, out_dtype):
    m, d = x.shape
    tm = _tile(m, 256)
    has_y = y is not None
    row = pl.BlockSpec((tm, d), lambda i: (i, 0))
    ins = [x] + ([y] if has_y else []) + [g.reshape(1, d)]
    in_specs = [row] * (2 if has_y else 1) + [pl.BlockSpec((1, d), lambda i: (0, 0))]
    out_shape = ([jax.ShapeDtypeStruct((m, d), F32)] if has_y else []) + [jax.ShapeDtypeStruct((m, d), out_dtype)]
    outs = pl.pallas_call(
        functools.partial(_addnorm_kernel, has_y=has_y, emit_x=has_y),
        grid=(m // tm,),
        in_specs=in_specs,
        out_specs=[row] * len(out_shape),
        out_shape=out_shape,
        compiler_params=_params(("arbitrary",), 48),
        name="add_rmsnorm",
    )(*ins)
    return (outs[0], outs[1]) if has_y else (x, outs[0])


def _final_norm_kernel(*refs, has_y, prompt_tiles):
    it = iter(refs)
    x_ref = next(it)
    y_ref = next(it) if has_y else None
    g_ref, p_ref, s_ref = next(it), next(it), next(it)
    i = pl.program_id(0)
    x = x_ref[...]
    if has_y:
        x = x + y_ref[...]
    n = _rms(x, g_ref[...])

    @pl.when(i < prompt_tiles)
    def _():
        p_ref[...] = n

    @pl.when(i >= prompt_tiles)
    def _():
        s_ref[...] = n


def _final_norm(x, y, g, mp):
    m, d = x.shape
    ms = m - mp
    tm = min(_tile(mp, 256), _tile(ms, 256))
    pt = mp // tm
    has_y = y is not None
    row = pl.BlockSpec((tm, d), lambda i: (i, 0))
    return pl.pallas_call(
        functools.partial(_final_norm_kernel, has_y=has_y, prompt_tiles=pt),
        grid=(m // tm,),
        in_specs=[row] * (2 if has_y else 1) + [pl.BlockSpec((1, d), lambda i: (0, 0))],
        out_specs=[pl.BlockSpec((tm, d), lambda i: (jnp.minimum(i, pt - 1), 0)),
                   pl.BlockSpec((tm, d), lambda i: (jnp.maximum(i - pt, 0), 0))],
        out_shape=[jax.ShapeDtypeStruct((mp, d), F32), jax.ShapeDtypeStruct((ms, d), F32)],
        compiler_params=_params(("arbitrary",), 48),
        name="final_rmsnorm",
    )(*([x] + ([y] if has_y else []) + [g.reshape(1, d)]))


def _mm_kernel(x_ref, w_ref, o_ref, *, transposed):
    w = w_ref[...].astype(BF16)
    dims = (((1,), (1,)), ((), ())) if transposed else (((1,), (0,)), ((), ()))
    o_ref[...] = lax.dot_general(x_ref[...], w, dims, preferred_element_type=F32).astype(o_ref.dtype)


def _mm(x, w, layer, col0, n, transposed=False, out_dtype=F32):
    m, k = x.shape
    tm = MM_ROW_TILE if m % MM_ROW_TILE == 0 else _tile(m, 1024)
    tn = _tile(n, 512)
    assert tn % LANES == 0
    if not transposed:
        assert col0 % tn == 0
        w_spec = pl.BlockSpec((None, k, tn), lambda i, j: (layer, 0, j + col0 // tn))
    elif col0 % tn == 0:
        w_spec = pl.BlockSpec((None, tn, k), lambda i, j: (layer, j + col0 // tn, 0))
    else:
        assert col0 % SUBLANES == 0 and col0 + n <= w.shape[1]
        w_spec = pl.BlockSpec((None, pl.Element(tn), pl.Element(k)),
                              lambda i, j: (layer, pl.multiple_of(col0 + j * tn, SUBLANES), 0))
    return pl.pallas_call(
        functools.partial(_mm_kernel, transposed=transposed),
        grid=(m // tm, n // tn),
        in_specs=[pl.BlockSpec((tm, k), lambda i, j: (i, 0), pipeline_mode=pl.Buffered(1)), w_spec],
        out_specs=pl.BlockSpec((tm, tn), lambda i, j: (i, j)),
        out_shape=jax.ShapeDtypeStruct((m, n), out_dtype),
        compiler_params=_params(("arbitrary", "arbitrary"), 56),
        name="matmul_nt" if transposed else "matmul",
    )(x, w)


def _log_sigmoid(z):
    y = -z
    return -(jnp.maximum(y, 0.0) + jnp.log1p(jnp.exp(-jnp.abs(y))))


def _lowrank_kernel(n_ref, w1_ref, w2_ref, b_ref, o_ref):
    glr = jnp.dot(n_ref[...], w1_ref[...], preferred_element_type=F32)
    z = jnp.dot(glr.astype(BF16), w2_ref[...], preferred_element_type=F32) + b_ref[...]
    o_ref[...] = _log_sigmoid(z) / GLA_TAU


def _lowrank_gate(n, w1, w2, b):
    m, d = n.shape
    gqk = w2.shape[1]
    tm = _tile(m, 512)
    return pl.pallas_call(
        _lowrank_kernel,
        grid=(m // tm,),
        in_specs=[
            pl.BlockSpec((tm, d), lambda i: (i, 0)),
            pl.BlockSpec(w1.shape, lambda i: (0, 0)),
            pl.BlockSpec(w2.shape, lambda i: (0, 0)),
            pl.BlockSpec((1, gqk), lambda i: (0, 0)),
        ],
        out_specs=pl.BlockSpec((tm, gqk), lambda i: (i, 0)),
        out_shape=jax.ShapeDtypeStruct((m, gqk), F32),
        compiler_params=_params(("arbitrary",), 32),
        name="gla_forget_gate",
    )(n, w1, w2, b)


def _rope(x, cos, sin):
    return x * cos + pltpu.roll(x, x.shape[-1] // 2, 1) * sin


def _swa_kernel(sink_ref, q_ref, k0_ref, k1_ref, k2_ref, v0_ref, v1_ref, v2_ref,
                c0_ref, c1_ref, c2_ref, s0_ref, s1_ref, s2_ref, o_ref, krot_ref, *, groups, cached):
    c = pl.program_id(1)
    hd = c2_ref.shape[-1]
    n_kv = k2_ref.shape[-1] // hd
    cosq, sinq = c2_ref[...], s2_ref[...]
    for kvh in range(n_kv):
        ks = slice(kvh * hd, (kvh + 1) * hd)
        k2 = _rope(k2_ref[:, ks], cosq, sinq)
        krot_ref[:, ks] = k2
        if cached:
            k0, k1 = k0_ref[:, ks], k1_ref[:, ks]
        else:
            k0 = _rope(k0_ref[:, ks], c0_ref[...], s0_ref[...])
            k1 = _rope(k1_ref[:, ks], c1_ref[...], s1_ref[...])
        kall = jnp.concatenate([k0, k1, k2], axis=0).astype(BF16)
        vall = jnp.concatenate([v0_ref[:, ks], v1_ref[:, ks], v2_ref[:, ks]], axis=0).astype(BF16)
        q0 = kvh * groups * hd
        qall = jnp.concatenate(
            [_rope(q_ref[:, q0 + g * hd:q0 + (g + 1) * hd], cosq, sinq) for g in range(groups)], axis=0).astype(BF16)
        s = lax.dot_general(qall, kall, (((1,), (1,)), ((), ())), preferred_element_type=F32) * hd ** -0.5
        if not cached:
            col = lax.broadcasted_iota(jnp.int32, s.shape, 1)
            s = jnp.where(col >= jnp.maximum(2 - c, 0) * CHUNK, s, NEG_INF)
        sink = jnp.concatenate(
            [jnp.full((CHUNK, 1), sink_ref[kvh * groups + g], F32) for g in range(groups)], axis=0)
        mx = jnp.maximum(jnp.max(s, axis=-1, keepdims=True), sink)
        p = jnp.exp(s - mx)
        den = jnp.sum(p, axis=-1, keepdims=True) + jnp.exp(sink - mx)
        p = p / den
        o = jnp.dot(p.astype(BF16), vall, preferred_element_type=F32)
        for g in range(groups):
            o_ref[:, q0 + g * hd:q0 + (g + 1) * hd] = o[g * CHUNK:(g + 1) * CHUNK].astype(o_ref.dtype)


def _swa(u, sink, cos, sin, cache_k, cache_v, *, n_seq, n_chunks, row0, n_kv, groups, hd, pos0, aq):
    cached = cache_k is not None
    akv = n_kv * hd
    assert aq % akv == 0
    rb0 = row0 // CHUNK
    pb0 = pos0 // CHUNK
    kcol, vcol = aq // akv, aq // akv + 1

    def rowblk(b, c):
        return rb0 + b * n_chunks + c

    def window(colblk):
        return [pl.BlockSpec((CHUNK, akv), lambda b, c, j=j: (rowblk(b, jnp.maximum(c - 2 + j, 0)), colblk))
                for j in range(2)]

    if cached:
        kprev = vprev = [pl.BlockSpec((CHUNK, akv), lambda b, c, j=j: (2 * b + j, 0)) for j in range(2)]
        kin, vin = [cache_k, cache_k], [cache_v, cache_v]
    else:
        kprev, vprev = window(kcol), window(vcol)
        kin = vin = [u, u]
    q_spec = pl.BlockSpec((CHUNK, aq), lambda b, c: (rowblk(b, c), 0))
    kcur = pl.BlockSpec((CHUNK, akv), lambda b, c: (rowblk(b, c), kcol))
    vcur = pl.BlockSpec((CHUNK, akv), lambda b, c: (rowblk(b, c), vcol))
    tab = [pl.BlockSpec((CHUNK, hd), lambda b, c, j=j: (pb0 + jnp.maximum(c - 2 + j, 0), 0)) for j in range(3)]
    out_rows = n_seq * n_chunks * CHUNK
    return pl.pallas_call(
        functools.partial(_swa_kernel, groups=groups, cached=cached),
        grid=(n_seq, n_chunks),
        in_specs=[pl.BlockSpec(memory_space=pltpu.SMEM), q_spec] + kprev + [kcur] + vprev + [vcur] + tab + tab,
        out_specs=[pl.BlockSpec((CHUNK, aq), lambda b, c: (b * n_chunks + c, 0)),
                   pl.BlockSpec((CHUNK, akv), lambda b, c: (b * n_chunks + c, 0))],
        out_shape=[jax.ShapeDtypeStruct((out_rows, aq), BF16), jax.ShapeDtypeStruct((out_rows, akv), F32)],
        compiler_params=_params(("arbitrary",) * 2, 32),
        name="swa_cached" if cached else "swa_prompt",
    )(sink, u, *kin, u, *vin, u, cos, cos, cos, sin, sin, sin)


def _cumsum_rows(x):
    row = lax.broadcasted_iota(jnp.int32, x.shape, 0)
    d = 1
    while d < x.shape[0]:
        x = x + jnp.where(row >= d, pltpu.roll(x, d, 0), 0.0)
        d *= 2
    return x


def _gla_kernel(*refs, heads, has_init):
    it = iter(refs)
    q_ref, k_ref, lg_ref, g_ref = (next(it) for _ in range(4))
    v_refs = [next(it) for _ in range(heads)]
    r_refs = [next(it) for _ in range(heads)]
    s0_ref = next(it) if has_init else None
    o_ref, sout_ref, st_ref = next(it), next(it), next(it)
    c = pl.program_id(1)
    dk = q_ref.shape[-1] // heads
    dv = v_refs[0].shape[-1]

    @pl.when(c == 0)
    def _():
        for h in range(heads):
            st_ref[h] = s0_ref[h].T if has_init else jnp.zeros((dv, dk), F32)

    tri = lax.broadcasted_iota(jnp.int32, (CHUNK, CHUNK), 0) >= lax.broadcasted_iota(jnp.int32, (CHUNK, CHUNK), 1)
    for h in range(heads):
        ks = slice(h * dk, (h + 1) * dk)
        b = _cumsum_rows(lg_ref[:, ks])
        b_last = b[CHUNK - 1:CHUNK, :]
        k = k_ref[:, ks]
        qe = (q_ref[:, ks] * dk ** -0.5 * jnp.exp(b)).astype(BF16)
        ke = (k * jnp.exp(-b)).astype(BF16)
        kd = (k * jnp.exp(b_last - b)).astype(BF16)
        v = v_refs[h][...].astype(BF16)
        att = lax.dot_general(qe, ke, (((1,), (1,)), ((), ())), preferred_element_type=F32)
        att = jnp.where(tri, att, 0.0)
        st = st_ref[h]
        o = lax.dot_general(qe, st.astype(BF16), (((1,), (1,)), ((), ())), preferred_element_type=F32)
        o = o + jnp.dot(att.astype(BF16), v, preferred_element_type=F32)
        st_new = jnp.exp(b_last) * st + lax.dot_general(v, kd, (((0,), (0,)), ((), ())), preferred_element_type=F32)
        st_ref[h] = st_new
        r = r_refs[h][...]
        o_ref[:, h * dv:(h + 1) * dv] = (_rms(o, g_ref[...]) * (r * jax.nn.sigmoid(r))).astype(o_ref.dtype)

    @pl.when(c == pl.num_programs(1) - 1)
    def _():
        for h in range(heads):
            sout_ref[h] = st_ref[h].T


def _gla(u, lg, g_norm, s0, *, n_seq, n_chunks, row0, heads, dk, dv, q_col, k_col, v_col, r_col):
    has_init = s0 is not None
    rb0 = row0 // CHUNK
    gqk = heads * dk
    assert q_col % gqk == 0 and k_col % gqk == 0 and v_col % dv == 0 and r_col % dv == 0

    def col(width, blk):
        return pl.BlockSpec((CHUNK, width), lambda b, c: (rb0 + b * n_chunks + c, blk))

    in_specs = ([col(gqk, q_col // gqk), col(gqk, k_col // gqk), col(gqk, 0), pl.BlockSpec((1, dv), lambda b, c: (0, 0))]
                + [col(dv, v_col // dv + h) for h in range(heads)] + [col(dv, r_col // dv + h) for h in range(heads)])
    ins = [u, u, lg, g_norm.reshape(1, dv)] + [u] * (2 * heads)
    if has_init:
        in_specs.append(pl.BlockSpec((None, heads, dk, dv), lambda b, c: (b, 0, 0, 0)))
        ins.append(s0)
    out_rows = n_seq * n_chunks * CHUNK
    return pl.pallas_call(
        functools.partial(_gla_kernel, heads=heads, has_init=has_init),
        grid=(n_seq, n_chunks),
        in_specs=in_specs,
        out_specs=[pl.BlockSpec((CHUNK, heads * dv), lambda b, c: (b * n_chunks + c, 0)),
                   pl.BlockSpec((None, heads, dk, dv), lambda b, c: (b, 0, 0, 0))],
        out_shape=[jax.ShapeDtypeStruct((out_rows, heads * dv), BF16),
                   jax.ShapeDtypeStruct((n_seq, heads, dk, dv), F32)],
        scratch_shapes=[pltpu.VMEM((heads, dv, dk), F32)],
        compiler_params=_params(("arbitrary",) * 2, 40),
        name="gla_init" if has_init else "gla_prompt",
    )(*ins)


def _softplus(y):
    return jnp.maximum(y, 0.0) + jnp.log1p(jnp.exp(-jnp.abs(y)))


def _lru_kernel(x_ref, xprev_ref, cprev_ref, y_ref, wc_ref, bc_ref, wa_ref, ba_ref, wx_ref, bx_ref, lam_ref, h0_ref,
                o_ref, hl_ref, h_ref, a_ref, u_ref, *, stream_start, conv_w):
    c = pl.program_id(1)
    t, w = x_ref.shape
    nblk, bw = wa_ref.shape[0], wa_ref.shape[1]

    @pl.when(c == 0)
    def _():
        h_ref[...] = h0_ref[...]

    prev = jnp.where(c == 0, cprev_ref[...], xprev_ref[...])
    ext = jnp.concatenate([prev, x_ref[...]], axis=0)
    xc = bc_ref[...]
    for j in range(conv_w):
        shift = conv_w - 1 - j
        xs = ext if shift == 0 else pltpu.roll(ext, shift, 0)
        xc = xc + xs[SUBLANES:] * wc_ref[j:j + 1, :]
    xb = xc.astype(BF16)
    ra = jnp.concatenate([jnp.dot(xb[:, n * bw:(n + 1) * bw], wa_ref[n], preferred_element_type=F32)
                          for n in range(nblk)], axis=1)
    ia = jnp.concatenate([jnp.dot(xb[:, n * bw:(n + 1) * bw], wx_ref[n], preferred_element_type=F32)
                          for n in range(nblk)], axis=1)
    r = jax.nn.sigmoid(ra + ba_ref[...])
    gate = jax.nn.sigmoid(ia + bx_ref[...])
    log_a = -LRU_C * r * _softplus(-lam_ref[...])
    a = jnp.exp(log_a)
    mult = jnp.sqrt(jnp.tanh(-log_a) * (a * a + 1.0))
    if stream_start:
        first = (lax.broadcasted_iota(jnp.int32, (t, w), 0) == 0) & (c == 0)
        mult = jnp.where(first, 1.0, mult)
        a = jnp.where(first, 0.0, a)
    a_ref[...] = a
    u_ref[...] = mult * gate * xc

    def step(i, h):
        h = a_ref[pl.ds(i, 1), :] * h + u_ref[pl.ds(i, 1), :]
        u_ref[pl.ds(i, 1), :] = h
        return h

    h = lax.fori_loop(0, t, step, h_ref[...], unroll=8)
    h_ref[...] = h
    hl_ref[...] = h
    o_ref[...] = (jax.nn.gelu(y_ref[...]) * u_ref[...]).astype(o_ref.dtype)


def _lru(uc, cprev8, h0, wc, bc, wa, ba, wx, bx, lam, *, n_seq, n_chunks, row0, width, stream_start):
    rb0 = row0 // CHUNK
    per = CHUNK // SUBLANES
    conv_w = wc.shape[0]
    vec = pl.BlockSpec((1, width), lambda b, c: (0, 0))
    blk = pl.BlockSpec(wa.shape, lambda b, c: (0, 0, 0))
    out_rows = n_seq * n_chunks * CHUNK
    return pl.pallas_call(
        functools.partial(_lru_kernel, stream_start=stream_start, conv_w=conv_w),
        grid=(n_seq, n_chunks),
        in_specs=[
            pl.BlockSpec((CHUNK, width), lambda b, c: (rb0 + b * n_chunks + c, 0)),
            pl.BlockSpec((SUBLANES, width), lambda b, c: (jnp.maximum((rb0 + b * n_chunks + c) * per - 1, 0), 0)),
            pl.BlockSpec((None, SUBLANES, width), lambda b, c: (b, 0, 0)),
            pl.BlockSpec((CHUNK, width), lambda b, c: (rb0 + b * n_chunks + c, 1)),
            pl.BlockSpec(wc.shape, lambda b, c: (0, 0)), vec, blk, vec, blk, vec, vec,
            pl.BlockSpec((None, 1, width), lambda b, c: (b, 0, 0)),
        ],
        out_specs=[pl.BlockSpec((CHUNK, width), lambda b, c: (b * n_chunks + c, 0)),
                   pl.BlockSpec((None, 1, width), lambda b, c: (b, 0, 0))],
        out_shape=[jax.ShapeDtypeStruct((out_rows, width), BF16), jax.ShapeDtypeStruct((n_seq, 1, width), F32)],
        scratch_shapes=[pltpu.VMEM((1, width), F32), pltpu.VMEM((CHUNK, width), F32), pltpu.VMEM((CHUNK, width), F32)],
        compiler_params=_params(("arbitrary",) * 2, 32),
        name="conv_rglru",
    )(uc, uc, cprev8, uc, wc, bc, wa, ba, wx, bx, lam, h0)


def _merge_kernel(n_ref, ap_ref, bp_ref, cp_ref, as_ref, bs_ref, cs_ref, wg0_ref, wg1_ref, wg2_ref, wb_ref,
                  bg0_ref, bg1_ref, bg2_ref, o_ref, wgbf_ref, wbbf_ref, *, prompt_tiles):
    i = pl.program_id(1)

    @pl.when(i == 0)
    def _():
        for b, wg_ref in enumerate((wg0_ref, wg1_ref, wg2_ref)):
            wgbf_ref[b] = wg_ref[...].astype(BF16)
            wbbf_ref[b] = wb_ref[b].astype(BF16)

    def merge(branch_refs):
        n = n_ref[...]
        acc = None
        for b, (br_ref, bg_ref) in enumerate(zip(branch_refs, (bg0_ref, bg1_ref, bg2_ref))):
            gate = jax.nn.sigmoid(jnp.dot(n, wgbf_ref[b], preferred_element_type=F32) + bg_ref[...])
            term = gate * jnp.dot(br_ref[...], wbbf_ref[b], preferred_element_type=F32)
            acc = term if acc is None else acc + term
        o_ref[...] = acc.astype(o_ref.dtype)

    @pl.when(i < prompt_tiles)
    def _():
        merge((ap_ref, bp_ref, cp_ref))

    @pl.when(i >= prompt_tiles)
    def _():
        merge((as_ref, bs_ref, cs_ref))


def _merge(n, prompt, sample, w_gate, b_gate, w_branch, layer):
    m, d = n.shape
    mp, bw = prompt[0].shape
    ms = sample[0].shape[0]
    nb = w_branch.shape[1]
    assert nb == 3 and mp + ms == m
    tm = min(_tile(mp, 512), _tile(ms, 512))
    tn = _tile(d, 256)
    nj = d // tn
    pt = mp // tm
    single = pl.Buffered(1)
    wg_specs = [pl.BlockSpec((None, d, tn), lambda j, i, b=b: (layer, 0, b * nj + j), pipeline_mode=single)
                for b in range(nb)]
    bg_specs = [pl.BlockSpec((None, 1, tn), lambda j, i, b=b: (layer, 0, b * nj + j)) for b in range(nb)]
    p_spec = pl.BlockSpec((tm, bw), lambda j, i: (jnp.minimum(i, pt - 1), 0))
    s_spec = pl.BlockSpec((tm, bw), lambda j, i: (jnp.maximum(i - pt, 0), 0), pipeline_mode=single)
    return pl.pallas_call(
        functools.partial(_merge_kernel, prompt_tiles=pt),
        grid=(nj, m // tm),
        in_specs=[pl.BlockSpec((tm, d), lambda j, i: (i, 0))] + [p_spec] * nb + [s_spec] * nb + wg_specs
        + [pl.BlockSpec((None, nb, bw, tn), lambda j, i: (layer, 0, 0, j), pipeline_mode=single)] + bg_specs,
        out_specs=pl.BlockSpec((tm, tn), lambda j, i: (i, j)),
        out_shape=jax.ShapeDtypeStruct((m, d), BF16),
        scratch_shapes=[pltpu.VMEM((nb, d, tn), BF16), pltpu.VMEM((nb, bw, tn), BF16)],
        compiler_params=_params(("arbitrary", "arbitrary"), 60),
        name="gated_merge",
    )(n, *prompt, *sample, w_gate, w_gate, w_gate, w_branch, b_gate, b_gate, b_gate)


def _ffn_kernel(te_ref, nv_ref, ns_ref, x_ref, wg_ref, wu_ref, wd_ref, o_ref, *bf_refs, sub):
    i, j = pl.program_id(0), pl.program_id(1)
    tm, d = o_ref.shape
    dc = min(d, FFN_DOWN_COLS)
    nsub = tm // sub
    used = ns_ref[i]

    def rows_body(rows, wg, wu, wd_cols):
        x = x_ref[rows, :]
        g = jnp.dot(x, wg, preferred_element_type=F32)
        u = jnp.dot(x, wu, preferred_element_type=F32)
        h = (g * jax.nn.sigmoid(g) * u).astype(BF16)

        @pl.when(j == 0)
        def _():
            for n0 in range(0, d, dc):
                o_ref[rows, n0:n0 + dc] = jnp.dot(h, wd_cols(n0), preferred_element_type=F32)

        @pl.when(j > 0)
        def _():
            for n0 in range(0, d, dc):
                o_ref[rows, n0:n0 + dc] += jnp.dot(h, wd_cols(n0), preferred_element_type=F32)

    @pl.when(used == nsub)
    def _():
        rows_body(slice(0, tm), wg_ref[...].astype(BF16), wu_ref[...].astype(BF16),
                  lambda n0: wd_ref[:, n0:n0 + dc].astype(BF16))

    if nsub > 1:
        wgbf_ref, wubf_ref, wdbf_ref = bf_refs

        @pl.when((used > 0) & (used < nsub))
        def _():
            wg, wu, wd = wg_ref[...].astype(BF16), wu_ref[...].astype(BF16), wd_ref[...].astype(BF16)
            wgbf_ref[...] = wg
            wubf_ref[...] = wu
            wdbf_ref[...] = wd
            rows_body(slice(0, sub), wg, wu, lambda n0: wd[:, n0:n0 + dc])
            for s in range(1, nsub - 1):
                @pl.when(s < used)
                def _(s=s):
                    rows_body(slice(s * sub, (s + 1) * sub), wgbf_ref[...], wubf_ref[...],
                              lambda n0: wdbf_ref[:, n0:n0 + dc])

    for s in range(nsub):
        @pl.when((s >= used) & (j == 0))
        def _(s=s):
            o_ref[s * sub:(s + 1) * sub, :] = jnp.zeros((sub, d), o_ref.dtype)


def _ffn(x, wg, wu, wd, tile_expert, n_valid, n_sub, tm, sub):
    r, d = x.shape
    f = wg.shape[2]
    tf = _tile(f, 256)
    nj = f // tf
    single = pl.Buffered(1)

    def last(i, nv):
        return jnp.maximum(jnp.minimum(i, nv[0] - 1), 0)

    def row(i, j, te, nv, ns):
        return (last(i, nv), 0)

    def fidx(i, j, nv):
        return jnp.where(i < nv[0], j, nj - 1)

    def up(i, j, te, nv, ns):
        return (te[last(i, nv)], 0, fidx(i, j, nv))

    def down(i, j, te, nv, ns):
        return (te[last(i, nv)], fidx(i, j, nv), 0)

    return pl.pallas_call(
        functools.partial(_ffn_kernel, sub=sub),
        grid_spec=pltpu.PrefetchScalarGridSpec(
            num_scalar_prefetch=3,
            grid=(r // tm, nj),
            in_specs=[pl.BlockSpec((tm, d), row, pipeline_mode=single),
                      pl.BlockSpec((None, d, tf), up), pl.BlockSpec((None, d, tf), up),
                      pl.BlockSpec((None, tf, d), down)],
            out_specs=pl.BlockSpec((tm, d), lambda i, j, te, nv, ns: (i, 0), pipeline_mode=single),
            scratch_shapes=([pltpu.VMEM((d, tf), BF16), pltpu.VMEM((d, tf), BF16), pltpu.VMEM((tf, d), BF16)]
                            if sub < tm else []),
        ),
        out_shape=jax.ShapeDtypeStruct((r, d), F32),
        compiler_params=_params(("arbitrary", "arbitrary"), 60),
        name="swiglu_ffn",
    )(tile_expert, n_valid, n_sub, x, wg, wu, wd)


def _router_kernel(x_ref, g_ref, w_ref, meta_ref, cnt_ref, carry_ref, *, n_experts):
    @pl.when(pl.program_id(0) == 0)
    def _():
        carry_ref[...] = jnp.zeros_like(carry_ref)

    tm = x_ref.shape[0]
    n = _rms(x_ref[...], g_ref[...])
    logits = jnp.dot(n, w_ref[...], preferred_element_type=F32, precision=lax.Precision.HIGHEST)
    lane = lax.broadcasted_iota(jnp.int32, logits.shape, 1).astype(F32)
    logits = jnp.where(lane < n_experts, logits, -jnp.inf)
    m1 = jnp.max(logits, axis=-1, keepdims=True)
    i1 = jnp.min(jnp.where(logits == m1, lane, float(LANES)), axis=-1, keepdims=True)
    rest = jnp.where(lane == i1, -jnp.inf, logits)
    m2 = jnp.max(rest, axis=-1, keepdims=True)
    i2 = jnp.min(jnp.where(rest == m2, lane, float(LANES)), axis=-1, keepdims=True)
    e2 = jnp.exp(m2 - m1)
    den = 1.0 + e2
    oh1 = lane == i1
    oh2 = lane == i2
    both = jnp.where(oh1 | oh2, 1.0, 0.0)
    rr = lax.broadcasted_iota(jnp.int32, (tm, tm), 0)
    cc = lax.broadcasted_iota(jnp.int32, (tm, tm), 1)
    below = jnp.where(rr > cc, 1.0, 0.0).astype(BF16)
    before = jnp.dot(below, both.astype(BF16), preferred_element_type=F32) + carry_ref[...]
    rank1 = jnp.sum(jnp.where(oh1, before, 0.0), axis=-1, keepdims=True)
    rank2 = jnp.sum(jnp.where(oh2, before, 0.0), axis=-1, keepdims=True)
    cols = (i1, i2, 1.0 / den, e2 / den, rank1, rank2)
    meta = jnp.zeros(logits.shape, F32)
    for idx, val in enumerate(cols):
        meta = jnp.where(lane == idx, val, meta)
    meta_ref[...] = meta
    total = carry_ref[...] + jnp.sum(both, axis=0, keepdims=True)
    carry_ref[...] = total
    cnt_ref[...] = total


def _router(x, g, w_pad, n_experts):
    m, d = x.shape
    tm = _tile(m, 256)
    return pl.pallas_call(
        functools.partial(_router_kernel, n_experts=n_experts),
        grid=(m // tm,),
        in_specs=[pl.BlockSpec((tm, d), lambda i: (i, 0)), pl.BlockSpec((1, d), lambda i: (0, 0)),
                  pl.BlockSpec(w_pad.shape, lambda i: (0, 0))],
        out_specs=[pl.BlockSpec((tm, LANES), lambda i: (i, 0)), pl.BlockSpec((1, LANES), lambda i: (0, 0))],
        out_shape=[jax.ShapeDtypeStruct((m, LANES), F32), jax.ShapeDtypeStruct((1, LANES), F32)],
        scratch_shapes=[pltpu.VMEM((1, LANES), F32)],
        compiler_params=_params(("arbitrary",), 40),
        name="moe_router",
    )(x, g.reshape(1, d), w_pad)


def _dispatch_kernel(used_ref, src_ref, nxt_ref, x_hbm, g_ref, o_ref, buf_ref, sem):
    i = pl.program_id(0)
    rows = o_ref.shape[0]
    slot = i % 2

    def copy(tbl_ref, r, s):
        return pltpu.make_async_copy(x_hbm.at[pl.ds(tbl_ref[0, r], 1)], buf_ref.at[s, pl.ds(r, 1)], sem.at[s])

    def start_all(tbl_ref, s):
        def body(r, _):
            copy(tbl_ref, r, s).start()
            return 0
        lax.fori_loop(0, rows, body, 0)

    def in_use(step):
        return step * rows < used_ref[0]

    @pl.when((i == 0) & in_use(0))
    def _():
        start_all(src_ref, 0)

    @pl.when((i + 1 < pl.num_programs(0)) & in_use(i + 1))
    def _():
        start_all(nxt_ref, 1 - slot)

    @pl.when(in_use(i))
    def _():
        def body(r, _):
            copy(src_ref, r, slot).wait()
            return 0
        lax.fori_loop(0, rows, body, 0)
        o_ref[...] = _rms(buf_ref[slot], g_ref[...]).astype(o_ref.dtype)

    @pl.when(jnp.logical_not(in_use(i)))
    def _():
        o_ref[...] = jnp.zeros_like(o_ref)


def _dispatch(x, g, src, n_used):
    m, d = x.shape
    n_rows = src.shape[0]
    rows = _tile(n_rows, 256)
    steps = n_rows // rows
    src3 = src.reshape(steps, 1, rows)
    return pl.pallas_call(
        _dispatch_kernel,
        grid=(steps,),
        in_specs=[pl.BlockSpec(memory_space=pltpu.SMEM),
                  pl.BlockSpec((None, 1, rows), lambda i: (i, 0, 0), memory_space=pltpu.SMEM),
                  pl.BlockSpec((None, 1, rows), lambda i: (jnp.minimum(i + 1, steps - 1), 0, 0),
                               memory_space=pltpu.SMEM),
                  pl.BlockSpec(memory_space=pl.ANY), pl.BlockSpec((1, d), lambda i: (0, 0))],
        out_specs=pl.BlockSpec((rows, d), lambda i: (i, 0)),
        out_shape=jax.ShapeDtypeStruct((n_rows, d), BF16),
        scratch_shapes=[pltpu.VMEM((2, rows, d), F32), pltpu.SemaphoreType.DMA((2,))],
        compiler_params=_params(("arbitrary",), 32),
        name="moe_dispatch",
    )(n_used, src3, src3, x, g.reshape(1, d))


def _combine_kernel(dest_ref, x_ref, meta_ref, y_hbm, o_ref, ya_ref, yb_ref, sem, *, rows):
    bufs = (ya_ref, yb_ref)

    def copy(r, k):
        return pltpu.make_async_copy(y_hbm.at[pl.ds(dest_ref[0, TOP_K * r + k], 1)], bufs[k].at[pl.ds(r, 1)], sem)

    def start(r, _):
        for k in range(TOP_K):
            copy(r, k).start()
        return 0

    def wait(r, _):
        for k in range(TOP_K):
            copy(r, k).wait()
        return 0

    lax.fori_loop(0, rows, start, 0)
    lax.fori_loop(0, rows, wait, 0)
    meta = meta_ref[...]
    o_ref[...] = x_ref[...] + (meta[:, 2:3] * ya_ref[...] + meta[:, 3:4] * yb_ref[...])


def _combine(x, meta, y, dest):
    m, d = x.shape
    rows = _tile(m, 128)
    dest3 = dest.reshape(m // rows, 1, rows * TOP_K)
    return pl.pallas_call(
        functools.partial(_combine_kernel, rows=rows),
        grid=(m // rows,),
        in_specs=[pl.BlockSpec((None, 1, rows * TOP_K), lambda i: (i, 0, 0), memory_space=pltpu.SMEM),
                  pl.BlockSpec((rows, d), lambda i: (i, 0)), pl.BlockSpec((rows, LANES), lambda i: (i, 0)),
                  pl.BlockSpec(memory_space=pl.ANY)],
        out_specs=pl.BlockSpec((rows, d), lambda i: (i, 0)),
        out_shape=jax.ShapeDtypeStruct((m, d), F32),
        scratch_shapes=[pltpu.VMEM((rows, d), F32), pltpu.VMEM((rows, d), F32), pltpu.SemaphoreType.DMA(())],
        compiler_params=_params(("arbitrary",), 32),
        name="moe_combine",
    )(dest3, x, meta, y)


def _moe(x, g, w_router, wg, wu, wd, tm):
    m, d = x.shape
    n_exp = w_router.shape[1]
    w_pad = jnp.zeros((d, LANES), F32).at[:, :n_exp].set(w_router)
    meta, counts = _router(x, g, w_pad, n_exp)
    expert = meta[:, 0:TOP_K].astype(jnp.int32)
    rank = meta[:, 4:4 + TOP_K].astype(jnp.int32)
    cnt = counts[0, :n_exp].astype(jnp.int32)
    padded = (cnt + tm - 1) // tm * tm
    ends = jnp.cumsum(padded)
    dest = (ends - padded)[expert] + rank
    n_tiles = (m * TOP_K) // tm + n_exp
    tile_ids = jnp.arange(n_tiles, dtype=jnp.int32)
    tile_expert = jnp.minimum(jnp.sum((ends // tm)[None, :] <= tile_ids[:, None], axis=1), n_exp - 1).astype(jnp.int32)
    n_valid = (ends[-1:] // tm).astype(jnp.int32)
    sub = FFN_SUB_ROWS if tm % FFN_SUB_ROWS == 0 else tm
    used = jnp.clip(cnt[tile_expert] - (tile_ids - ((ends - padded) // tm)[tile_expert]) * tm, 0, tm)
    n_sub = jnp.where(tile_ids < n_valid[0], (used + sub - 1) // sub, 0).astype(jnp.int32)
    token = jnp.broadcast_to(jnp.arange(m, dtype=jnp.int32)[:, None], (m, TOP_K))
    src = (jnp.arange(n_tiles * tm, dtype=jnp.int32) % m).at[dest.reshape(-1)].set(token.reshape(-1))
    ns = _dispatch(x, g, src, ends[-1:].astype(jnp.int32))
    ys = _ffn(ns, wg, wu, wd, tile_expert, n_valid, n_sub, tm, sub)
    return _combine(x, meta, ys, dest)


def kernel(x_prompt, x_sample, cache_swa_k, cache_swa_v, state_gla, state_conv, state_lru, ln_mix, w_in, attn_sink, w_gla_lr2, b_gla_gate, g_gla_norm, w_conv, b_conv, w_lru_a, b_lru_a, w_lru_x, b_lru_x, lru_lambda, w_branch, w_gate, b_gate, w_out, ln_ffn, w_ff_gate, w_ff_up, w_ff_down, w_router, w_moe_gate, w_moe_up, w_moe_down, ln_final):
    bp, seq, d = x_prompt.shape
    bs, dseq, _ = x_sample.shape
    depth = ln_mix.shape[0]
    _, _, window, n_kv, hd = cache_swa_k.shape
    n_heads = attn_sink.shape[1]
    groups = n_heads // n_kv
    _, _, gh, dk, dv = state_gla.shape
    lowrank = w_gla_lr2.shape[1]
    conv_w = w_conv.shape[1]
    width = state_lru.shape[2]
    aq, akv, gqk, gv = n_heads * hd, n_kv * hd, gh * dk, gh * dv
    assert dseq == CHUNK and seq % CHUNK == 0 and window == 2 * CHUNK and hd == LANES and lowrank <= LANES
    mp, ms = bp * seq, bs * dseq
    m = mp + ms
    nc = seq // CHUNK
    main = aq + 2 * akv + 2 * gqk + 2 * gv
    c_col = main + lowrank
    assert w_in.shape[2] == c_col + 2 * width

    w_in_t = jnp.swapaxes(w_in, 1, 2)
    w_lr = jnp.zeros((depth, d, LANES), BF16).at[:, :, :lowrank].set(
        jnp.swapaxes(w_in_t[:, main:c_col, :], 1, 2).astype(BF16))
    w_lr2 = jnp.zeros((depth, LANES, gqk), BF16).at[:, :lowrank].set(w_gla_lr2.astype(BF16))
    cache_k2 = cache_swa_k.reshape(depth, bs * window, akv)
    cache_v2 = cache_swa_v.reshape(depth, bs * window, akv)
    conv_prev_s = jnp.pad(state_conv, ((0, 0), (0, 0), (SUBLANES - (conv_w - 1), 0), (0, 0)))
    conv_prev_p = jnp.zeros((bp, SUBLANES, width), F32)
    wa_bf, wx_bf = w_lru_a.astype(BF16), w_lru_x.astype(BF16)

    n_pos = max(seq, PAST_LEN + dseq)
    cos, sin = _rope_tables(n_pos, hd)

    x = jnp.concatenate([x_prompt.reshape(mp, d), x_sample.reshape(ms, d)], axis=0)
    y = None
    outs = {k: [] for k in ("pk", "pv", "pg", "pc", "pl", "sk", "sv", "sg", "sc", "sl")}
    tm_ffn = FFN_ROW_TILE if m % FFN_ROW_TILE == 0 else _tile(m, FFN_ROW_TILE)
    tm_moe = MOE_ROW_TILE if (m * TOP_K) % MOE_ROW_TILE == 0 else _tile(m, MOE_ROW_TILE)
    for l in range(depth):
        x, n = _addnorm(x, y, ln_mix[l], BF16)
        u = _mm(n, w_in_t, l, 0, main, transposed=True)
        uc = _mm(n, w_in_t, l, c_col, 2 * width, transposed=True)
        lg = _lowrank_gate(n, w_lr[l], w_lr2[l], b_gla_gate[l].reshape(1, gqk))

        swa = functools.partial(_swa, u, attn_sink[l], cos, sin, n_kv=n_kv, groups=groups, hd=hd, aq=aq)
        oa_p, kr_p = swa(None, None, n_seq=bp, n_chunks=nc, row0=0, pos0=0)
        oa_s, kr_s = swa(cache_k2[l], cache_v2[l], n_seq=bs, n_chunks=1, row0=mp, pos0=PAST_LEN)

        gla = functools.partial(_gla, u, lg, g_gla_norm[l], heads=gh, dk=dk, dv=dv, q_col=aq + 2 * akv,
                                k_col=aq + 2 * akv + gqk, v_col=aq + 2 * akv + 2 * gqk,
                                r_col=aq + 2 * akv + 2 * gqk + gv)
        ob_p, sg_p = gla(None, n_seq=bp, n_chunks=nc, row0=0)
        ob_s, sg_s = gla(state_gla[l], n_seq=bs, n_chunks=1, row0=mp)

        lru = functools.partial(_lru, uc, wc=w_conv[l], bc=b_conv[l].reshape(1, width), wa=wa_bf[l],
                                ba=b_lru_a[l].reshape(1, width), wx=wx_bf[l], bx=b_lru_x[l].reshape(1, width),
                                lam=lru_lambda[l].reshape(1, width), width=width)
        oc_p, hl_p = lru(conv_prev_p, jnp.zeros((bp, 1, width), F32), n_seq=bp, n_chunks=nc, row0=0,
                         stream_start=True)
        oc_s, hl_s = lru(conv_prev_s[l], state_lru[l].reshape(bs, 1, width), n_seq=bs, n_chunks=1, row0=mp,
                         stream_start=False)

        merged = _merge(n, (oa_p, ob_p, oc_p), (oa_s, ob_s, oc_s), w_gate, b_gate.reshape(depth, 1, -1), w_branch, l)
        y = _mm(merged, w_out, l, 0, d)
        x, n2 = _addnorm(x, y, ln_ffn[l], BF16)
        if l % 2 == 0:
            i = l // 2
            tiles = m // tm_ffn
            y = _ffn(n2, w_ff_gate[i:i + 1], w_ff_up[i:i + 1], w_ff_down[i:i + 1], jnp.zeros((tiles,), jnp.int32),
                     jnp.full((1,), tiles, jnp.int32), jnp.ones((tiles,), jnp.int32), tm_ffn, tm_ffn)
        else:
            i = l // 2
            x = _moe(x, ln_ffn[l], w_router[i], w_moe_gate[i], w_moe_up[i], w_moe_down[i], tm_moe)
            y = None

        def tail_rows(arr, b, t, keep_rows, c0, c1):
            return jnp.stack([arr[i * t + t - keep_rows:(i + 1) * t, c0:c1] for i in range(b)])

        v0, v1 = aq + akv, aq + 2 * akv
        keep = window - dseq
        outs["pk"].append(tail_rows(kr_p, bp, seq, window, 0, akv).reshape(bp, window, n_kv, hd))
        outs["pv"].append(tail_rows(u, bp, seq, window, v0, v1).reshape(bp, window, n_kv, hd))
        outs["pg"].append(sg_p)
        outs["pc"].append(tail_rows(uc, bp, seq, conv_w - 1, 0, width))
        outs["pl"].append(hl_p.reshape(bp, width))
        outs["sk"].append(jnp.concatenate([cache_swa_k[l][:, window - keep:], kr_s.reshape(bs, dseq, n_kv, hd)], 1))
        outs["sv"].append(jnp.concatenate([cache_swa_v[l][:, window - keep:],
                                           u[mp:, v0:v1].reshape(bs, dseq, n_kv, hd)], 1))
        outs["sg"].append(sg_s)
        outs["sc"].append(uc[mp:, :width].reshape(bs, dseq, width)[:, dseq - (conv_w - 1):])
        outs["sl"].append(hl_s.reshape(bs, width))

    yp, ys = _final_norm(x, y, ln_final, mp)
    st = {k: jnp.stack(v) for k, v in outs.items()}
    return (yp.reshape(bp, seq, d), ys.reshape(bs, dseq, d),
            st["pk"], st["pv"], st["pg"], st["pc"], st["pl"],
            st["sk"], st["sv"], st["sg"], st["sc"], st["sl"])
```
